```python
import math
import jax, jax.numpy as jnp
from jax import lax
import numpy as np

D_MODEL = 4096
BATCH = 4
SEQ = 4096
DEPTH = 2

HEAD_DIM = 128
MIX_WIDTH = D_MODEL
ATT_HEADS = (3 * MIX_WIDTH) // (8 * HEAD_DIM)
ATT_KV_HEADS = ATT_HEADS // 3
WINDOW = 128
ATT_BLOCK = WINDOW
RET_HEADS = (3 * MIX_WIDTH) // (8 * HEAD_DIM)
RET_CHUNK = 128
CONV_CH = MIX_WIDTH - (ATT_HEADS + RET_HEADS) * HEAD_DIM
CONV_WIDTH = 31
D_FF = 11008
FFN_CONV_WIDTH = 3

ATT_Q_COLS = ATT_HEADS * HEAD_DIM
ATT_KV_COLS = ATT_KV_HEADS * HEAD_DIM
CONV_IN_COLS = 2 * CONV_CH
RET_COLS = RET_HEADS * HEAD_DIM
IN_COLS = ATT_Q_COLS + 2 * ATT_KV_COLS + CONV_IN_COLS + 4 * RET_COLS

NORM_EPS = 1e-6
NEG_INF = -1e30

kernel_name = "hybrid_swa_conformer_retention_convffn"


def rmsnorm(x, g):
    xf = x.astype(jnp.float32)
    y = xf * lax.rsqrt(jnp.mean(xf * xf, axis=-1, keepdims=True) + NORM_EPS)
    return (y * g.astype(jnp.float32)).astype(x.dtype)


def layernorm(x, g, b):
    xf = x.astype(jnp.float32)
    mu = jnp.mean(xf, axis=-1, keepdims=True)
    xc = xf - mu
    y = xc * lax.rsqrt(jnp.mean(xc * xc, axis=-1, keepdims=True) + NORM_EPS)
    return (y * g.astype(jnp.float32) + b.astype(jnp.float32)).astype(x.dtype)


def causal_depthwise_conv(x, w, b):
    kw, c = w.shape
    y = lax.conv_general_dilated(
        x, w[:, None, :].astype(x.dtype), window_strides=(1,), padding=[(kw - 1, 0)],
        dimension_numbers=("NWC", "WIO", "NWC"), feature_group_count=c)
    return y + b.astype(x.dtype)


def _pow2_slopes(n):
    start = 2.0 ** (-8.0 / n)
    return [start ** (i + 1) for i in range(n)]


def alibi_slopes(n):
    if math.log2(n).is_integer():
        return _pow2_slopes(n)
    c = 2 ** math.floor(math.log2(n))
    return _pow2_slopes(c) + alibi_slopes(2 * c)[0::2][: n - c]


def sliding_window_attention(q, k, v, sinks, slopes):
    b, s, h, d = q.shape
    hkv = k.shape[2]
    g = h // hkv
    nb = s // ATT_BLOCK
    qb = q.reshape(b, nb, ATT_BLOCK, hkv, g, d)

    def with_prev(t):
        tb = t.reshape(b, nb, ATT_BLOCK, hkv, d)
        prev = jnp.pad(tb, ((0, 0), (1, 0), (0, 0), (0, 0), (0, 0)))[:, :-1]
        return jnp.concatenate([prev, tb], axis=2)

    kc, vc = with_prev(k), with_prev(v)
    scores = jnp.einsum("bnqhgd,bnkhd->bnhgqk", qb, kc).astype(jnp.float32) * (d ** -0.5)
    qi = jnp.arange(ATT_BLOCK)[:, None]
    kj = jnp.arange(2 * ATT_BLOCK)[None, :]
    dist = qi + ATT_BLOCK - kj
    in_window = (dist >= 0) & (dist < WINDOW)
    key_exists = ~((jnp.arange(nb)[:, None, None] == 0) & (kj < ATT_BLOCK)[None])
    valid = in_window[None] & key_exists
    slopes_hg = slopes.reshape(hkv, g).astype(jnp.float32)
    scores = scores - slopes_hg[:, :, None, None] * dist.astype(jnp.float32)
    scores = jnp.where(valid[None, :, None, None], scores, NEG_INF)
    sink = jnp.broadcast_to(sinks.reshape(hkv, g).astype(jnp.float32)[None, None, :, :, None, None],
                            scores.shape[:-1] + (1,))
    probs = jax.nn.softmax(jnp.concatenate([scores, sink], axis=-1), axis=-1)[..., :-1]
    out = jnp.einsum("bnhgqk,bnkhd->bnqhgd", probs.astype(v.dtype), vc)
    return out.reshape(b, s, h * d)


def retention(q, k, v, log_gamma):
    b, s, h, d = q.shape
    n = s // RET_CHUNK
    qc = q.astype(jnp.float32).reshape(b, n, RET_CHUNK, h, d)
    kc = k.astype(jnp.float32).reshape(b, n, RET_CHUNK, h, d) * (d ** -0.5)
    vc = v.astype(jnp.float32).reshape(b, n, RET_CHUNK, h, d)
    pos = jnp.arange(RET_CHUNK, dtype=jnp.float32)
    rel = pos[:, None] - pos[None, :]
    intra_decay = jnp.where(rel >= 0, jnp.exp(log_gamma[:, None, None] * jnp.maximum(rel, 0.0)), 0.0)
    scores = jnp.einsum("bnihd,bnjhd->bnhij", qc, kc) * intra_decay
    intra = jnp.einsum("bnhij,bnjhd->bnihd", scores, vc)
    q_decay = jnp.exp(log_gamma[:, None] * (pos + 1.0)[None])
    k_decay = jnp.exp(log_gamma[:, None] * (RET_CHUNK - 1.0 - pos)[None])
    chunk_decay = jnp.exp(log_gamma * RET_CHUNK)
    kv = jnp.einsum("bnjhd,bnjhe,hj->nbhde", kc, vc, k_decay)

    def step(state, kv_n):
        return state * chunk_decay[None, :, None, None] + kv_n, state

    _, prev = lax.scan(step, jnp.zeros((b, h, d, d), jnp.float32), kv)
    cross = jnp.einsum("bnihd,nbhde,hi->bnihe", qc, prev, q_decay)
    return (intra + cross).reshape(b, s, h, d)


def head_groupnorm(x):
    mu = jnp.mean(x, axis=-1, keepdims=True)
    xc = x - mu
    return xc * lax.rsqrt(jnp.mean(xc * xc, axis=-1, keepdims=True) + NORM_EPS)


def conformer_conv(u, dw_w, dw_b, ln_g, ln_b, pw_w):
    a, gate = jnp.split(u, 2, axis=-1)
    y = a * jax.nn.sigmoid(gate)
    y = causal_depthwise_conv(y, dw_w, dw_b)
    y = layernorm(y, ln_g, ln_b)
    y = jax.nn.silu(y)
    return y @ pw_w


def conv_ffn(h, w_in, dw_w, dw_b, w_out):
    u = causal_depthwise_conv(h @ w_in, dw_w, dw_b)
    gate, val = jnp.split(u, 2, axis=-1)
    return (jax.nn.gelu(gate, approximate=True) * val) @ w_out


def setup_inputs(seed: int = 0) -> dict:
    key = jax.random.key(seed)
    ks = jax.random.split(key, 20)
    L = DEPTH

    def nrm(k, shape, scale):
        return jax.random.normal(k, shape, jnp.float32) * scale

    def gain(k, shape):
        return 1.0 + 0.1 * jax.random.normal(k, shape, jnp.float32)

    return {
        "x": nrm(ks[0], (BATCH, SEQ, D_MODEL), 1.0),
        "mix_pre_g": gain(ks[1], (L, D_MODEL)),
        "mix_post_g": gain(ks[2], (L, D_MODEL)),
        "ffn_pre_g": gain(ks[3], (L, D_MODEL)),
        "ffn_post_g": gain(ks[4], (L, D_MODEL)),
        "w_in": nrm(ks[5], (L, D_MODEL, IN_COLS), D_MODEL ** -0.5),
        "att_sinks": nrm(ks[6], (L, ATT_HEADS), 0.5),
        "conv_dw_w": nrm(ks[7], (L, CONV_WIDTH, CONV_CH), CONV_WIDTH ** -0.5),
        "conv_dw_b": nrm(ks[8], (L, CONV_CH), 0.02),
        "conv_ln_g": gain(ks[9], (L, CONV_CH)),
        "conv_ln_b": nrm(ks[10], (L, CONV_CH), 0.02),
        "conv_pw_w": nrm(ks[11], (L, CONV_CH, CONV_CH), CONV_CH ** -0.5),
        "w_out": nrm(ks[12], (L, MIX_WIDTH, D_MODEL), MIX_WIDTH ** -0.5),
        "ffn_w_in": nrm(ks[13], (L, D_MODEL, 2 * D_FF), D_MODEL ** -0.5),
        "ffn_dw_w": nrm(ks[14], (L, FFN_CONV_WIDTH, 2 * D_FF), FFN_CONV_WIDTH ** -0.5),
        "ffn_dw_b": nrm(ks[15], (L, 2 * D_FF), 0.02),
        "ffn_w_out": nrm(ks[16], (L, D_FF, D_MODEL), D_FF ** -0.5),
    }


def reference(x, mix_pre_g, mix_post_g, ffn_pre_g, ffn_post_g, w_in, att_sinks,
              conv_dw_w, conv_dw_b, conv_ln_g, conv_ln_b, conv_pw_w, w_out,
              ffn_w_in, ffn_dw_w, ffn_dw_b, ffn_w_out):
    b, s, _ = x.shape
    slopes = jnp.asarray(alibi_slopes(ATT_HEADS), dtype=jnp.float32)
    log_gamma = jnp.log1p(-jnp.exp2(-5.0 - jnp.arange(RET_HEADS, dtype=jnp.float32)))
    split_points = list(np.cumsum([ATT_Q_COLS, ATT_KV_COLS, ATT_KV_COLS, CONV_IN_COLS,
                                   RET_COLS, RET_COLS, RET_COLS]))
    for l in range(DEPTH):
        h = rmsnorm(x, mix_pre_g[l])
        proj = h @ w_in[l]
        qa, ka, va, cu, qr, kr, vr, gr = jnp.split(proj, split_points, axis=-1)
        att = sliding_window_attention(
            qa.reshape(b, s, ATT_HEADS, HEAD_DIM),
            ka.reshape(b, s, ATT_KV_HEADS, HEAD_DIM),
            va.reshape(b, s, ATT_KV_HEADS, HEAD_DIM),
            att_sinks[l], slopes)
        conv = conformer_conv(cu, conv_dw_w[l], conv_dw_b[l], conv_ln_g[l], conv_ln_b[l],
                              conv_pw_w[l])
        ret = retention(qr.reshape(b, s, RET_HEADS, HEAD_DIM),
                        kr.reshape(b, s, RET_HEADS, HEAD_DIM),
                        vr.reshape(b, s, RET_HEADS, HEAD_DIM), log_gamma)
        ret = head_groupnorm(ret).reshape(b, s, RET_COLS)
        ret = (ret * jax.nn.silu(gr.astype(jnp.float32))).astype(x.dtype)
        mixed = jnp.concatenate([att.astype(x.dtype), conv.astype(x.dtype), ret], axis=-1) @ w_out[l]
        x = x + rmsnorm(mixed, mix_post_g[l])
        h = rmsnorm(x, ffn_pre_g[l])
        x = x + rmsnorm(conv_ffn(h, ffn_w_in[l], ffn_dw_w[l], ffn_dw_b[l], ffn_w_out[l]), ffn_post_g[l])
    return x
```

```python
import functools
import math

import jax
import jax.numpy as jnp
from jax import lax
from jax.experimental import pallas as pl
from jax.experimental.pallas import tpu as pltpu

HEAD_DIM = 128
WINDOW = 128
RET_CHUNK = 128
NORM_EPS = 1e-6
NEG_INF = -1e30
GQA_RATIO = 3

F32 = jnp.float32
BF16 = jnp.bfloat16

V7X_VMEM_BYTES = 64 * 1024 * 1024
VMEM_LIMIT_BYTES = 56 * 1024 * 1024
SUBLANES_BF16 = 16


def _params(*semantics):
    return pltpu.CompilerParams(dimension_semantics=semantics,
                                vmem_limit_bytes=VMEM_LIMIT_BYTES)


def _tile(n, pref):
    t = min(n, pref)
    while n % t:
        t //= 2
    return t


def _rms(v):
    return v * lax.rsqrt(jnp.mean(v * v, axis=-1, keepdims=True) + NORM_EPS)


def _prenorm_kernel(x_ref, g_ref, h_ref):
    h_ref[...] = (_rms(x_ref[...]) * g_ref[...]).astype(h_ref.dtype)


def _resid_prenorm_kernel(x_ref, y_ref, gpost_ref, gpre_ref, xo_ref, h_ref):
    x = x_ref[...] + _rms(y_ref[...].astype(F32)) * gpost_ref[...]
    xo_ref[...] = x
    h_ref[...] = (_rms(x) * gpre_ref[...]).astype(h_ref.dtype)


def _resid_kernel(x_ref, y_ref, gpost_ref, xo_ref):
    xo_ref[...] = x_ref[...] + _rms(y_ref[...].astype(F32)) * gpost_ref[...]


def _row_spec(tr, d):
    return pl.BlockSpec((tr, d), lambda i: (i, 0))


def _gain_spec(d):
    return pl.BlockSpec((1, d), lambda i: (0, 0))


def prenorm(x, g):
    m, d = x.shape
    tr = _tile(m, 256)
    return pl.pallas_call(
        _prenorm_kernel, grid=(m // tr,),
        in_specs=[_row_spec(tr, d), _gain_spec(d)],
        out_specs=_row_spec(tr, d),
        out_shape=jax.ShapeDtypeStruct((m, d), BF16),
        compiler_params=_params("parallel"), name="prenorm",
    )(x, g.reshape(1, d))


def resid_prenorm(x, y, gpost, gpre):
    m, d = x.shape
    tr = _tile(m, 256)
    return pl.pallas_call(
        _resid_prenorm_kernel, grid=(m // tr,),
        in_specs=[_row_spec(tr, d), _row_spec(tr, d), _gain_spec(d), _gain_spec(d)],
        out_specs=[_row_spec(tr, d), _row_spec(tr, d)],
        out_shape=[jax.ShapeDtypeStruct((m, d), F32), jax.ShapeDtypeStruct((m, d), BF16)],
        compiler_params=_params("parallel"), name="resid_prenorm",
    )(x, y, gpost.reshape(1, d), gpre.reshape(1, d))


def resid(x, y, gpost):
    m, d = x.shape
    tr = _tile(m, 256)
    return pl.pallas_call(
        _resid_kernel, grid=(m // tr,),
        in_specs=[_row_spec(tr, d), _row_spec(tr, d), _gain_spec(d)],
        out_specs=_row_spec(tr, d),
        out_shape=jax.ShapeDtypeStruct((m, d), F32),
        compiler_params=_params("parallel"), name="resid",
    )(x, y, gpost.reshape(1, d))


def _matmul_kernel(a_ref, b_ref, o_ref):
    o_ref[...] = jnp.dot(a_ref[...], b_ref[...], preferred_element_type=F32).astype(o_ref.dtype)


def in_proj(h, w):
    m, k = h.shape
    n = w.shape[1]
    tm, tn = _tile(m, 1024), _tile(n, 512)
    return pl.pallas_call(
        _matmul_kernel, grid=(m // tm, n // tn),
        in_specs=[pl.BlockSpec((tm, k), lambda i, j: (i, 0)),
                  pl.BlockSpec((k, tn), lambda i, j: (0, j))],
        out_specs=pl.BlockSpec((tm, tn), lambda i, j: (i, j)),
        out_shape=jax.ShapeDtypeStruct((m, n), BF16),
        compiler_params=_params("parallel", "arbitrary"), name="in_proj",
    )(h, w)


def _attn_kernel(sink_ref, slope_ref, q_ref, kc_ref, vc_ref, kp_ref, vp_ref, o_ref, *, n_kv):
    n = pl.program_id(1)
    blk = WINDOW
    qi = lax.broadcasted_iota(jnp.int32, (blk, 2 * blk), 0)
    kj = lax.broadcasted_iota(jnp.int32, (blk, 2 * blk), 1)
    dist = qi + blk - kj
    valid = (dist >= 0) & (dist < WINDOW) & ((kj >= blk) | (n > 0))
    distf = dist.astype(F32)
    scale = HEAD_DIM ** -0.5
    for hk in range(n_kv):
        ks = slice(hk * HEAD_DIM, (hk + 1) * HEAD_DIM)
        k = jnp.concatenate([kp_ref[:, ks], kc_ref[:, ks]], axis=0)
        v = jnp.concatenate([vp_ref[:, ks], vc_ref[:, ks]], axis=0)
        for g in range(GQA_RATIO):
            h = hk * GQA_RATIO + g
            hs = slice(h * HEAD_DIM, (h + 1) * HEAD_DIM)
            s = lax.dot_general(q_ref[:, hs], k, (((1,), (1,)), ((), ())),
                                preferred_element_type=F32)
            s = s * scale - slope_ref[h] * distf
            s = jnp.where(valid, s, NEG_INF)
            sink = sink_ref[h]
            mx = jnp.maximum(jnp.max(s, axis=-1, keepdims=True), sink)
            p = jnp.exp(s - mx)
            denom = jnp.sum(p, axis=-1, keepdims=True) + jnp.exp(sink - mx)
            out = jnp.dot(p.astype(BF16), v, preferred_element_type=F32) / denom
            o_ref[:, hs] = out.astype(o_ref.dtype)


def attention(proj, sinks, slopes, batch, seq, n_heads):
    m = proj.shape[0]
    n_kv = n_heads // GQA_RATIO
    qw, kw = n_heads * HEAD_DIM, n_kv * HEAD_DIM
    nb = seq // WINDOW
    k_blk, v_blk = qw // kw, qw // kw + 1
    cur = lambda b, n: b * nb + n
    prev = lambda b, n: b * nb + jnp.maximum(n - 1, 0)
    smem = pl.BlockSpec(memory_space=pltpu.SMEM)
    return pl.pallas_call(
        functools.partial(_attn_kernel, n_kv=n_kv), grid=(batch, nb),
        in_specs=[smem, smem,
                  pl.BlockSpec((WINDOW, qw), lambda b, n: (cur(b, n), 0)),
                  pl.BlockSpec((WINDOW, kw), lambda b, n: (cur(b, n), k_blk)),
                  pl.BlockSpec((WINDOW, kw), lambda b, n: (cur(b, n), v_blk)),
                  pl.BlockSpec((WINDOW, kw), lambda b, n: (prev(b, n), k_blk)),
                  pl.BlockSpec((WINDOW, kw), lambda b, n: (prev(b, n), v_blk))],
        out_specs=pl.BlockSpec((WINDOW, qw), lambda b, n: (cur(b, n), 0)),
        out_shape=jax.ShapeDtypeStruct((m, qw), BF16),
        compiler_params=_params("parallel", "parallel"), name="swa_attention",
    )(sinks, slopes, proj, proj, proj, proj, proj)


CONV_HALO = 32


def _conformer_kernel(a0_ref, a1_ref, g0_ref, g1_ref, dww_ref, dwb_ref, lng_ref, lnb_ref,
                      pw_ref, o_ref, ybuf_ref, cbuf_ref, *, ts, width, rows_per_chunk):
    half = a0_ref.shape[1]
    ch = 2 * half

    @pl.when(pl.program_id(1) == 0)
    def _():
        ybuf_ref[0:CONV_HALO, :] = jnp.zeros((CONV_HALO, ch), F32)

    for c, (a_ref, g_ref) in enumerate(((a0_ref, g0_ref), (a1_ref, g1_ref))):
        a = a_ref[...].astype(F32)
        g = g_ref[...].astype(F32)
        ybuf_ref[CONV_HALO:, c * half:(c + 1) * half] = a * jax.nn.sigmoid(g)

    base = CONV_HALO - (width - 1)
    lanes = 128

    for r0 in range(0, ts, rows_per_chunk):
        for c0 in range(0, ch, lanes):
            acc = jnp.broadcast_to(dwb_ref[:, c0:c0 + lanes], (rows_per_chunk, lanes))
            for k in range(width):
                acc = acc + dww_ref[k:k + 1, c0:c0 + lanes] * ybuf_ref[r0 + base + k:r0 + base + k + rows_per_chunk,
                                                                      c0:c0 + lanes]
            cbuf_ref[r0:r0 + rows_per_chunk, c0:c0 + lanes] = acc
    ybuf_ref[0:CONV_HALO, :] = ybuf_ref[ts:ts + CONV_HALO, :]

    y = cbuf_ref[...]
    mu = jnp.mean(y, axis=-1, keepdims=True)
    yc = y - mu
    yn = yc * lax.rsqrt(jnp.mean(yc * yc, axis=-1, keepdims=True) + NORM_EPS)
    yn = yn * lng_ref[...] + lnb_ref[...]
    act = yn * jax.nn.sigmoid(yn)
    o_ref[...] = jnp.dot(act.astype(BF16), pw_ref[...], preferred_element_type=F32).astype(o_ref.dtype)


def conformer(proj, dw_w, dw_b, ln_g, ln_b, pw_w, batch, seq, col0):
    m = proj.shape[0]
    width, ch = dw_w.shape
    half = ch // 2
    assert width - 1 <= CONV_HALO and col0 % half == 0
    ts = _tile(seq, 512)
    ns = seq // ts
    cb = col0 // half
    row = lambda b, s: b * ns + s
    col_spec = lambda j: pl.BlockSpec((ts, half), lambda b, s: (row(b, s), cb + j))
    full = lambda shape: pl.BlockSpec(shape, lambda b, s: (0, 0))
    return pl.pallas_call(
        functools.partial(_conformer_kernel, ts=ts, width=width, rows_per_chunk=64),
        grid=(batch, ns),
        in_specs=[col_spec(0), col_spec(1), col_spec(2), col_spec(3),
                  full((width, ch)), full((1, ch)), full((1, ch)), full((1, ch)), full((ch, ch))],
        out_specs=pl.BlockSpec((ts, ch), lambda b, s: (row(b, s), 0)),
        out_shape=jax.ShapeDtypeStruct((m, ch), BF16),
        scratch_shapes=[pltpu.VMEM((ts + CONV_HALO, ch), F32), pltpu.VMEM((ts, ch), F32)],
        compiler_params=_params("parallel", "arbitrary"), name="conformer_conv",
    )(proj, proj, proj, proj, dw_w, dw_b.reshape(1, ch), ln_g.reshape(1, ch), ln_b.reshape(1, ch), pw_w)


def _retention_kernel(lg_ref, q_ref, k_ref, v_ref, g_ref, o_ref, state_ref, *, n_heads):
    c = RET_CHUNK

    @pl.when(pl.program_id(1) == 0)
    def _():
        state_ref[...] = jnp.zeros(state_ref.shape, F32)

    row = lax.broadcasted_iota(jnp.int32, (c, c), 0)
    col = lax.broadcasted_iota(jnp.int32, (c, c), 1)
    rel = (row - col).astype(F32)
    causal = row >= col
    pos = lax.broadcasted_iota(jnp.int32, (c, 1), 0).astype(F32)
    chunk_len = jnp.full((1, 1), float(c), F32)
    scale = HEAD_DIM ** -0.5
    for h in range(n_heads):
        hs = slice(h * HEAD_DIM, (h + 1) * HEAD_DIM)
        lg = lg_ref[h]
        q, k, v = q_ref[:, hs], k_ref[:, hs], v_ref[:, hs]
        intra_decay = jnp.where(causal, jnp.exp(lg * jnp.maximum(rel, 0.0)), 0.0)
        s = lax.dot_general(q, k, (((1,), (1,)), ((), ())), preferred_element_type=F32)
        s = s * (scale * intra_decay)
        intra = jnp.dot(s.astype(BF16), v, preferred_element_type=F32)
        state = state_ref[h]
        q_decay = jnp.exp(lg * (pos + 1.0))
        cross = q_decay * jnp.dot(q, state.astype(BF16), preferred_element_type=F32)
        k_decay = jnp.exp(lg * (c - 1.0 - pos)) * scale
        kd = (k.astype(F32) * k_decay).astype(BF16)
        kv = lax.dot_general(kd, v, (((0,), (0,)), ((), ())), preferred_element_type=F32)
        state_ref[h] = state * jnp.exp(lg * chunk_len) + kv
        r = intra + cross
        mu = jnp.mean(r, axis=-1, keepdims=True)
        rc = r - mu
        rn = rc * lax.rsqrt(jnp.mean(rc * rc, axis=-1, keepdims=True) + NORM_EPS)
        gate = g_ref[:, hs].astype(F32)
        o_ref[:, hs] = (rn * (gate * jax.nn.sigmoid(gate))).astype(o_ref.dtype)


def retention(proj, log_gamma, batch, seq, n_heads, col0):
    m = proj.shape[0]
    w = n_heads * HEAD_DIM
    assert col0 % w == 0
    cb = col0 // w
    nc = seq // RET_CHUNK
    row = lambda b, n: b * nc + n
    spec = lambda j: pl.BlockSpec((RET_CHUNK, w), lambda b, n: (row(b, n), cb + j))
    return pl.pallas_call(
        functools.partial(_retention_kernel, n_heads=n_heads), grid=(batch, nc),
        in_specs=[pl.BlockSpec(memory_space=pltpu.SMEM), spec(0), spec(1), spec(2), spec(3)],
        out_specs=pl.BlockSpec((RET_CHUNK, w), lambda b, n: (row(b, n), 0)),
        out_shape=jax.ShapeDtypeStruct((m, w), BF16),
        scratch_shapes=[pltpu.VMEM((n_heads, HEAD_DIM, HEAD_DIM), F32)],
        compiler_params=_params("parallel", "arbitrary"), name="retention",
    )(log_gamma, proj, proj, proj, proj)


def _mix_out_kernel(att_ref, conv_ref, ret_ref, w_ref, o_ref):
    ka, kc = att_ref.shape[1], conv_ref.shape[1]
    acc = jnp.dot(att_ref[...], w_ref[0:ka, :], preferred_element_type=F32)
    acc += jnp.dot(conv_ref[...], w_ref[ka:ka + kc, :], preferred_element_type=F32)
    acc += jnp.dot(ret_ref[...], w_ref[ka + kc:, :], preferred_element_type=F32)
    o_ref[...] = acc.astype(o_ref.dtype)


def mix_out(att, conv, ret, w):
    m = att.shape[0]
    k, n = w.shape
    tm, tn = _tile(m, 1024), _tile(n, 512)
    lhs = lambda a: pl.BlockSpec((tm, a.shape[1]), lambda i, j: (i, 0))
    return pl.pallas_call(
        _mix_out_kernel, grid=(m // tm, n // tn),
        in_specs=[lhs(att), lhs(conv), lhs(ret), pl.BlockSpec((k, tn), lambda i, j: (0, j))],
        out_specs=pl.BlockSpec((tm, tn), lambda i, j: (i, j)),
        out_shape=jax.ShapeDtypeStruct((m, n), BF16),
        compiler_params=_params("parallel", "arbitrary"), name="mix_out",
    )(att, conv, ret, w)


FFN_HALO = SUBLANES_BF16


def _gelu_tanh(x):
    return 0.5 * x * (1.0 + jnp.tanh(math.sqrt(2.0 / math.pi) * (x + 0.044715 * (x * x * x))))


def _ffn_in_kernel(h_ref, halo_ref, wg_ref, wv_ref, dwg_ref, dwv_ref, bg_ref, bv_ref, o_ref,
                   lhs_ref, ug_ref, uv_ref, *, tm, width, tiles_per_seq):
    i = pl.program_id(0)

    @pl.when(pl.program_id(1) == 0)
    def _():
        seq_start = (i % tiles_per_seq) == 0
        halo = halo_ref[...]
        lhs_ref[0:FFN_HALO, :] = jnp.where(seq_start, jnp.zeros_like(halo), halo)
        lhs_ref[FFN_HALO:, :] = h_ref[...]

    lhs = lhs_ref[...]
    ug_ref[...] = jnp.dot(lhs, wg_ref[...], preferred_element_type=F32)
    uv_ref[...] = jnp.dot(lhs, wv_ref[...], preferred_element_type=F32)

    def conv(u_ref, dw_ref, b_ref):
        acc = b_ref[...]
        for k in range(width):
            acc = acc + dw_ref[k:k + 1, :] * u_ref[pl.ds(FFN_HALO - (width - 1) + k, tm), :]
        return acc

    gate = conv(ug_ref, dwg_ref, bg_ref)
    val = conv(uv_ref, dwv_ref, bv_ref)
    o_ref[...] = (_gelu_tanh(gate) * val).astype(o_ref.dtype)


def ffn_in(h, w, dw_w, dw_b, seq):
    m, k = h.shape
    width, n2 = dw_w.shape
    d_ff = n2 // 2
    assert width - 1 <= FFN_HALO
    tm, tn = _tile(seq, 1024), _tile(d_ff, 256)
    nj = d_ff // tn
    halo_blocks = tm // FFN_HALO
    return pl.pallas_call(
        functools.partial(_ffn_in_kernel, tm=tm, width=width, tiles_per_seq=seq // tm),
        grid=(m // tm, nj),
        in_specs=[pl.BlockSpec((tm, k), lambda i, j: (i, 0)),
                  pl.BlockSpec((FFN_HALO, k), lambda i, j: (jnp.maximum(i * halo_blocks - 1, 0), 0)),
                  pl.BlockSpec((k, tn), lambda i, j: (0, j)),
                  pl.BlockSpec((k, tn), lambda i, j: (0, j + nj)),
                  pl.BlockSpec((width, tn), lambda i, j: (0, j)),
                  pl.BlockSpec((width, tn), lambda i, j: (0, j + nj)),
                  pl.BlockSpec((1, tn), lambda i, j: (0, j)),
                  pl.BlockSpec((1, tn), lambda i, j: (0, j + nj))],
        out_specs=pl.BlockSpec((tm, tn), lambda i, j: (i, j)),
        out_shape=jax.ShapeDtypeStruct((m, d_ff), BF16),
        scratch_shapes=[pltpu.VMEM((tm + FFN_HALO, k), BF16),
                        pltpu.VMEM((tm + FFN_HALO, tn), F32),
                        pltpu.VMEM((tm + FFN_HALO, tn), F32)],
        compiler_params=_params("parallel", "arbitrary"), name="ffn_in",
    )(h, h, w, w, dw_w, dw_w, dw_b.reshape(1, n2), dw_b.reshape(1, n2))


def ffn_out(a, w):
    m, k = a.shape
    n = w.shape[1]
    tm, tn = _tile(m, 512), _tile(n, 256)
    return pl.pallas_call(
        _matmul_kernel, grid=(m // tm, n // tn),
        in_specs=[pl.BlockSpec((tm, k), lambda i, j: (i, 0)),
                  pl.BlockSpec((k, tn), lambda i, j: (0, j))],
        out_specs=pl.BlockSpec((tm, tn), lambda i, j: (i, j)),
        out_shape=jax.ShapeDtypeStruct((m, n), BF16),
        compiler_params=_params("parallel", "arbitrary"), name="ffn_out",
    )(a, w)


def _pow2_slopes(n):
    start = 2.0 ** (-8.0 / n)
    return [start ** (i + 1) for i in range(n)]


def _alibi_slopes(n):
    if math.log2(n).is_integer():
        return _pow2_slopes(n)
    c = 2 ** math.floor(math.log2(n))
    return _pow2_slopes(c) + _alibi_slopes(2 * c)[0::2][: n - c]


def kernel(x, mix_pre_g, mix_post_g, ffn_pre_g, ffn_post_g, w_in, att_sinks, conv_dw_w, conv_dw_b,
           conv_ln_g, conv_ln_b, conv_pw_w, w_out, ffn_w_in, ffn_dw_w, ffn_dw_b, ffn_w_out):
    b, s, d = x.shape
    depth = w_in.shape[0]
    att_heads = att_sinks.shape[1]
    conv_ch = conv_dw_b.shape[1]
    att_q_cols = att_heads * HEAD_DIM
    att_kv_cols = att_q_cols // GQA_RATIO
    ret_heads = (w_in.shape[2] - att_q_cols - 2 * att_kv_cols - 2 * conv_ch) // (4 * HEAD_DIM)
    conv_col0 = att_q_cols + 2 * att_kv_cols
    ret_col0 = conv_col0 + 2 * conv_ch
    assert s % WINDOW == 0 and s % RET_CHUNK == 0

    slopes = jnp.asarray(_alibi_slopes(att_heads), dtype=F32)
    log_gamma = jnp.log1p(-jnp.exp2(-5.0 - jnp.arange(ret_heads, dtype=F32)))

    xf = x.reshape(b * s, d)
    y = None
    for l in range(depth):
        if l == 0:
            h = prenorm(xf, mix_pre_g[l])
        else:
            xf, h = resid_prenorm(xf, y, ffn_post_g[l - 1], mix_pre_g[l])
        proj = in_proj(h, w_in[l].astype(BF16))
        att = attention(proj, att_sinks[l], slopes, b, s, att_heads)
        conv = conformer(proj, conv_dw_w[l], conv_dw_b[l], conv_ln_g[l], conv_ln_b[l],
                         conv_pw_w[l].astype(BF16), b, s, conv_col0)
        ret = retention(proj, log_gamma, b, s, ret_heads, ret_col0)
        y = mix_out(att, conv, ret, w_out[l].astype(BF16))
        xf, h = resid_prenorm(xf, y, mix_post_g[l], ffn_pre_g[l])
        a = ffn_in(h, ffn_w_in[l].astype(BF16), ffn_dw_w[l], ffn_dw_b[l], s)
        y = ffn_out(a, ffn_w_out[l].astype(BF16))
    xf = resid(xf, y, ffn_post_g[depth - 1])
    return xf.reshape(b, s, d)
```

```python
import functools
import math

import jax
import jax.numpy as jnp
from jax import lax
from jax.experimental import pallas as pl
from jax.experimental.pallas import tpu as pltpu

HEAD_DIM = 128
WINDOW = 128
RET_CHUNK = 128
NORM_EPS = 1e-6
NEG_INF = -1e30
GQA_RATIO = 3

F32 = jnp.float32
BF16 = jnp.bfloat16

V7X_VMEM_BYTES = 64 * 1024 * 1024
VMEM_LIMIT_BYTES = 56 * 1024 * 1024
SUBLANES_BF16 = 16


def _params(*semantics, flags=None):
    return pltpu.CompilerParams(dimension_semantics=semantics,
                                vmem_limit_bytes=VMEM_LIMIT_BYTES, flags=flags)


def _tile(n, pref):
    t = min(n, pref)
    while n % t:
        t //= 2
    return t


def _rms(v):
    return v * lax.rsqrt(jnp.mean(v * v, axis=-1, keepdims=True) + NORM_EPS)


def _prenorm_kernel(x_ref, g_ref, h_ref):
    h_ref[...] = (_rms(x_ref[...]) * g_ref[...]).astype(h_ref.dtype)


def _resid_prenorm_kernel(x_ref, y_ref, gpost_ref, gpre_ref, xo_ref, h_ref):
    x = x_ref[...] + _rms(y_ref[...].astype(F32)) * gpost_ref[...]
    xo_ref[...] = x
    h_ref[...] = (_rms(x) * gpre_ref[...]).astype(h_ref.dtype)


def _resid_kernel(x_ref, y_ref, gpost_ref, xo_ref):
    xo_ref[...] = x_ref[...] + _rms(y_ref[...].astype(F32)) * gpost_ref[...]


def _row_spec(tr, d):
    return pl.BlockSpec((tr, d), lambda i: (i, 0))


def _gain_spec(d):
    return pl.BlockSpec((1, d), lambda i: (0, 0))


def prenorm(x, g):
    m, d = x.shape
    tr = _tile(m, 256)
    return pl.pallas_call(
        _prenorm_kernel, grid=(m // tr,),
        in_specs=[_row_spec(tr, d), _gain_spec(d)],
        out_specs=_row_spec(tr, d),
        out_shape=jax.ShapeDtypeStruct((m, d), BF16),
        compiler_params=_params("parallel"), name="prenorm",
    )(x, g.reshape(1, d))


def resid_prenorm(x, y, gpost, gpre):
    m, d = x.shape
    tr = _tile(m, 256)
    return pl.pallas_call(
        _resid_prenorm_kernel, grid=(m // tr,),
        in_specs=[_row_spec(tr, d), _row_spec(tr, d), _gain_spec(d), _gain_spec(d)],
        out_specs=[_row_spec(tr, d), _row_spec(tr, d)],
        out_shape=[jax.ShapeDtypeStruct((m, d), F32), jax.ShapeDtypeStruct((m, d), BF16)],
        compiler_params=_params("parallel"), name="resid_prenorm",
    )(x, y, gpost.reshape(1, d), gpre.reshape(1, d))


def resid(x, y, gpost):
    m, d = x.shape
    tr = _tile(m, 256)
    return pl.pallas_call(
        _resid_kernel, grid=(m // tr,),
        in_specs=[_row_spec(tr, d), _row_spec(tr, d), _gain_spec(d)],
        out_specs=_row_spec(tr, d),
        out_shape=jax.ShapeDtypeStruct((m, d), F32),
        compiler_params=_params("parallel"), name="resid",
    )(x, y, gpost.reshape(1, d))


def _matmul_kernel(a_ref, b_ref, o_ref):
    o_ref[...] = jnp.dot(a_ref[...], b_ref[...], preferred_element_type=F32).astype(o_ref.dtype)


def in_proj(h, w):
    m, k = h.shape
    n = w.shape[1]
    tm, tn = _tile(m, 1024), _tile(n, 512)
    return pl.pallas_call(
        _matmul_kernel, grid=(m // tm, n // tn),
        in_specs=[pl.BlockSpec((tm, k), lambda i, j: (i, 0)),
                  pl.BlockSpec((k, tn), lambda i, j: (0, j))],
        out_specs=pl.BlockSpec((tm, tn), lambda i, j: (i, j)),
        out_shape=jax.ShapeDtypeStruct((m, n), BF16),
        compiler_params=_params("parallel", "arbitrary"), name="in_proj",
    )(h, w)


def _attn_kernel(sink_ref, slope_ref, q_ref, kc_ref, vc_ref, kp_ref, vp_ref, o_ref, *, n_kv):
    n = pl.program_id(1)
    blk = WINDOW
    qi = lax.broadcasted_iota(jnp.int32, (blk, 2 * blk), 0)
    kj = lax.broadcasted_iota(jnp.int32, (blk, 2 * blk), 1)
    dist = qi + blk - kj
    valid = (dist >= 0) & (dist < WINDOW) & ((kj >= blk) | (n > 0))
    distf = dist.astype(F32)
    scale = HEAD_DIM ** -0.5
    for hk in range(n_kv):
        ks = slice(hk * HEAD_DIM, (hk + 1) * HEAD_DIM)
        k = jnp.concatenate([kp_ref[:, ks], kc_ref[:, ks]], axis=0)
        v = jnp.concatenate([vp_ref[:, ks], vc_ref[:, ks]], axis=0)
        for g in range(GQA_RATIO):
            h = hk * GQA_RATIO + g
            hs = slice(h * HEAD_DIM, (h + 1) * HEAD_DIM)
            s = lax.dot_general(q_ref[:, hs], k, (((1,), (1,)), ((), ())),
                                preferred_element_type=F32)
            s = s * scale - slope_ref[h] * distf
            s = jnp.where(valid, s, NEG_INF)
            sink = sink_ref[h]
            mx = jnp.maximum(jnp.max(s, axis=-1, keepdims=True), sink)
            p = jnp.exp(s - mx)
            denom = jnp.sum(p, axis=-1, keepdims=True) + jnp.exp(sink - mx)
            out = jnp.dot(p.astype(BF16), v, preferred_element_type=F32) / denom
            o_ref[:, hs] = out.astype(o_ref.dtype)


def attention(proj, sinks, slopes, batch, seq, n_heads):
    m = proj.shape[0]
    n_kv = n_heads // GQA_RATIO
    qw, kw = n_heads * HEAD_DIM, n_kv * HEAD_DIM
    nb = seq // WINDOW
    k_blk, v_blk = qw // kw, qw // kw + 1
    cur = lambda b, n: b * nb + n
    prev = lambda b, n: b * nb + jnp.maximum(n - 1, 0)
    smem = pl.BlockSpec(memory_space=pltpu.SMEM)
    return pl.pallas_call(
        functools.partial(_attn_kernel, n_kv=n_kv), grid=(batch, nb),
        in_specs=[smem, smem,
                  pl.BlockSpec((WINDOW, qw), lambda b, n: (cur(b, n), 0)),
                  pl.BlockSpec((WINDOW, kw), lambda b, n: (cur(b, n), k_blk)),
                  pl.BlockSpec((WINDOW, kw), lambda b, n: (cur(b, n), v_blk)),
                  pl.BlockSpec((WINDOW, kw), lambda b, n: (prev(b, n), k_blk)),
                  pl.BlockSpec((WINDOW, kw), lambda b, n: (prev(b, n), v_blk))],
        out_specs=pl.BlockSpec((WINDOW, qw), lambda b, n: (cur(b, n), 0)),
        out_shape=jax.ShapeDtypeStruct((m, qw), BF16),
        compiler_params=_params("parallel", "parallel"), name="swa_attention",
    )(sinks, slopes, proj, proj, proj, proj, proj)


CONV_HALO = 32


def _conformer_kernel(a0_ref, a1_ref, g0_ref, g1_ref, dww_ref, dwb_ref, lng_ref, lnb_ref,
                      pw_ref, o_ref, ybuf_ref, cbuf_ref, *, ts, width, rows_per_chunk):
    half = a0_ref.shape[1]
    ch = 2 * half

    @pl.when(pl.program_id(1) == 0)
    def _():
        ybuf_ref[0:CONV_HALO, :] = jnp.zeros((CONV_HALO, ch), F32)

    for c, (a_ref, g_ref) in enumerate(((a0_ref, g0_ref), (a1_ref, g1_ref))):
        a = a_ref[...].astype(F32)
        g = g_ref[...].astype(F32)
        ybuf_ref[CONV_HALO:, c * half:(c + 1) * half] = a * jax.nn.sigmoid(g)

    base = CONV_HALO - (width - 1)
    lanes = 128

    for r0 in range(0, ts, rows_per_chunk):
        for c0 in range(0, ch, lanes):
            acc = jnp.broadcast_to(dwb_ref[:, c0:c0 + lanes], (rows_per_chunk, lanes))
            for k in range(width):
                acc = acc + dww_ref[k:k + 1, c0:c0 + lanes] * ybuf_ref[r0 + base + k:r0 + base + k + rows_per_chunk,
                                                                      c0:c0 + lanes]
            cbuf_ref[r0:r0 + rows_per_chunk, c0:c0 + lanes] = acc
    ybuf_ref[0:CONV_HALO, :] = ybuf_ref[ts:ts + CONV_HALO, :]

    y = cbuf_ref[...]
    mu = jnp.mean(y, axis=-1, keepdims=True)
    yc = y - mu
    yn = yc * lax.rsqrt(jnp.mean(yc * yc, axis=-1, keepdims=True) + NORM_EPS)
    yn = yn * lng_ref[...] + lnb_ref[...]
    act = yn * jax.nn.sigmoid(yn)
    o_ref[...] = jnp.dot(act.astype(BF16), pw_ref[...], preferred_element_type=F32).astype(o_ref.dtype)


def conformer(proj, dw_w, dw_b, ln_g, ln_b, pw_w, batch, seq, col0):
    m = proj.shape[0]
    width, ch = dw_w.shape
    half = ch // 2
    assert width - 1 <= CONV_HALO and col0 % half == 0
    ts = _tile(seq, 512)
    ns = seq // ts
    cb = col0 // half
    row = lambda b, s: b * ns + s
    col_spec = lambda j: pl.BlockSpec((ts, half), lambda b, s: (row(b, s), cb + j))
    full = lambda shape: pl.BlockSpec(shape, lambda b, s: (0, 0))
    return pl.pallas_call(
        functools.partial(_conformer_kernel, ts=ts, width=width, rows_per_chunk=64),
        grid=(batch, ns),
        in_specs=[col_spec(0), col_spec(1), col_spec(2), col_spec(3),
                  full((width, ch)), full((1, ch)), full((1, ch)), full((1, ch)), full((ch, ch))],
        out_specs=pl.BlockSpec((ts, ch), lambda b, s: (row(b, s), 0)),
        out_shape=jax.ShapeDtypeStruct((m, ch), BF16),
        scratch_shapes=[pltpu.VMEM((ts + CONV_HALO, ch), F32), pltpu.VMEM((ts, ch), F32)],
        compiler_params=_params("parallel", "arbitrary"), name="conformer_conv",
    )(proj, proj, proj, proj, dw_w, dw_b.reshape(1, ch), ln_g.reshape(1, ch), ln_b.reshape(1, ch), pw_w)


def _retention_kernel(lg_ref, q_ref, k_ref, v_ref, g_ref, o_ref, state_ref, *, n_heads):
    c = RET_CHUNK

    @pl.when(pl.program_id(1) == 0)
    def _():
        state_ref[...] = jnp.zeros(state_ref.shape, F32)

    row = lax.broadcasted_iota(jnp.int32, (c, c), 0)
    col = lax.broadcasted_iota(jnp.int32, (c, c), 1)
    rel = (row - col).astype(F32)
    causal = row >= col
    pos = lax.broadcasted_iota(jnp.int32, (c, 1), 0).astype(F32)
    chunk_len = jnp.full((1, 1), float(c), F32)
    scale = HEAD_DIM ** -0.5
    for h in range(n_heads):
        hs = slice(h * HEAD_DIM, (h + 1) * HEAD_DIM)
        lg = lg_ref[h]
        q, k, v = q_ref[:, hs], k_ref[:, hs], v_ref[:, hs]
        intra_decay = jnp.where(causal, jnp.exp(lg * jnp.maximum(rel, 0.0)), 0.0)
        s = lax.dot_general(q, k, (((1,), (1,)), ((), ())), preferred_element_type=F32)
        s = s * (scale * intra_decay)
        intra = jnp.dot(s.astype(BF16), v, preferred_element_type=F32)
        state = state_ref[h]
        q_decay = jnp.exp(lg * (pos + 1.0))
        cross = q_decay * jnp.dot(q, state.astype(BF16), preferred_element_type=F32)
        k_decay = jnp.exp(lg * (c - 1.0 - pos)) * scale
        kd = (k.astype(F32) * k_decay).astype(BF16)
        kv = lax.dot_general(kd, v, (((0,), (0,)), ((), ())), preferred_element_type=F32)
        state_ref[h] = state * jnp.exp(lg * chunk_len) + kv
        r = intra + cross
        mu = jnp.mean(r, axis=-1, keepdims=True)
        rc = r - mu
        rn = rc * lax.rsqrt(jnp.mean(rc * rc, axis=-1, keepdims=True) + NORM_EPS)
        gate = g_ref[:, hs].astype(F32)
        o_ref[:, hs] = (rn * (gate * jax.nn.sigmoid(gate))).astype(o_ref.dtype)


def retention(proj, log_gamma, batch, seq, n_heads, col0):
    m = proj.shape[0]
    w = n_heads * HEAD_DIM
    assert col0 % w == 0
    cb = col0 // w
    nc = seq // RET_CHUNK
    row = lambda b, n: b * nc + n
    spec = lambda j: pl.BlockSpec((RET_CHUNK, w), lambda b, n: (row(b, n), cb + j))
    return pl.pallas_call(
        functools.partial(_retention_kernel, n_heads=n_heads), grid=(batch, nc),
        in_specs=[pl.BlockSpec(memory_space=pltpu.SMEM), spec(0), spec(1), spec(2), spec(3)],
        out_specs=pl.BlockSpec((RET_CHUNK, w), lambda b, n: (row(b, n), 0)),
        out_shape=jax.ShapeDtypeStruct((m, w), BF16),
        scratch_shapes=[pltpu.VMEM((n_heads, HEAD_DIM, HEAD_DIM), F32)],
        compiler_params=_params("parallel", "arbitrary"), name="retention",
    )(log_gamma, proj, proj, proj, proj)


def _mix_out_kernel(att_ref, conv_ref, ret_ref, w_ref, o_ref):
    ka, kc = att_ref.shape[1], conv_ref.shape[1]
    acc = jnp.dot(att_ref[...], w_ref[0:ka, :], preferred_element_type=F32)
    acc += jnp.dot(conv_ref[...], w_ref[ka:ka + kc, :], preferred_element_type=F32)
    acc += jnp.dot(ret_ref[...], w_ref[ka + kc:, :], preferred_element_type=F32)
    o_ref[...] = acc.astype(o_ref.dtype)


def mix_out(att, conv, ret, w):
    m = att.shape[0]
    k, n = w.shape
    tm, tn = _tile(m, 1024), _tile(n, 512)
    lhs = lambda a: pl.BlockSpec((tm, a.shape[1]), lambda i, j: (i, 0))
    return pl.pallas_call(
        _mix_out_kernel, grid=(m // tm, n // tn),
        in_specs=[lhs(att), lhs(conv), lhs(ret), pl.BlockSpec((k, tn), lambda i, j: (0, j))],
        out_specs=pl.BlockSpec((tm, tn), lambda i, j: (i, j)),
        out_shape=jax.ShapeDtypeStruct((m, n), BF16),
        compiler_params=_params("parallel", "arbitrary"), name="mix_out",
    )(att, conv, ret, w)


FFN_HALO = SUBLANES_BF16
EPILOGUE_ROWS = 32


def _gelu_tanh(x):
    return 0.5 * x * (1.0 + jnp.tanh(math.sqrt(2.0 / math.pi) * (x + 0.044715 * (x * x * x))))


def _ffn_in_kernel(h_ref, halo_ref, wg_ref, wv_ref, dwg_ref, dwv_ref, bg_ref, bv_ref, o_ref,
                   lhs_ref, ug0_ref, uv0_ref, ug1_ref, uv1_ref, *, tm, width, tiles_per_seq, nj):
    i = pl.program_id(0)
    j = pl.program_id(1)
    slots = ((ug0_ref, uv0_ref), (ug1_ref, uv1_ref))

    def matmul(slot):
        ug_ref, uv_ref = slots[slot]
        lhs = lhs_ref[...]
        ug_ref[...] = jnp.dot(lhs, wg_ref[...], preferred_element_type=F32)
        uv_ref[...] = jnp.dot(lhs, wv_ref[...], preferred_element_type=F32)

    def conv(u_ref, dw_ref, b_ref, r0, rows):
        acc = b_ref[...]
        for k in range(width):
            acc = acc + dw_ref[k:k + 1, :] * u_ref[pl.ds(r0 + FFN_HALO - (width - 1) + k, rows), :]
        return acc

    def epilogue(slot):
        ug_ref, uv_ref = slots[slot]
        for r0 in range(0, tm, EPILOGUE_ROWS):
            gate = conv(ug_ref, dwg_ref, bg_ref, r0, EPILOGUE_ROWS)
            val = conv(uv_ref, dwv_ref, bv_ref, r0, EPILOGUE_ROWS)
            o_ref[r0:r0 + EPILOGUE_ROWS, :] = (_gelu_tanh(gate) * val).astype(o_ref.dtype)

    @pl.when(j == 0)
    def _():
        seq_start = (i % tiles_per_seq) == 0
        halo = halo_ref[...]
        lhs_ref[0:FFN_HALO, :] = jnp.where(seq_start, jnp.zeros_like(halo), halo)
        lhs_ref[FFN_HALO:, :] = h_ref[...]
        matmul(0)

    for parity in (0, 1):
        @pl.when((j > 0) & (j < nj) & (j % 2 == parity))
        def _():
            epilogue(1 - parity)
            matmul(parity)

    @pl.when(j == nj)
    def _():
        epilogue((nj - 1) % 2)


def ffn_in(h, w, dw_w, dw_b, seq):
    m, k = h.shape
    width, n2 = dw_w.shape
    d_ff = n2 // 2
    assert width - 1 <= FFN_HALO
    tm, tn = _tile(seq, 1024), _tile(d_ff, 256)
    nj = d_ff // tn
    halo_blocks = tm // FFN_HALO
    mm_tile = lambda j: jnp.minimum(j, nj - 1)
    ep_tile = lambda j: jnp.maximum(j - 1, 0)
    u_scratch = pltpu.VMEM((tm + FFN_HALO, tn), F32)
    return pl.pallas_call(
        functools.partial(_ffn_in_kernel, tm=tm, width=width, tiles_per_seq=seq // tm, nj=nj),
        grid=(m // tm, nj + 1),
        in_specs=[pl.BlockSpec((tm, k), lambda i, j: (i, 0)),
                  pl.BlockSpec((FFN_HALO, k), lambda i, j: (jnp.maximum(i * halo_blocks - 1, 0), 0)),
                  pl.BlockSpec((k, tn), lambda i, j: (0, mm_tile(j))),
                  pl.BlockSpec((k, tn), lambda i, j: (0, mm_tile(j) + nj)),
                  pl.BlockSpec((width, tn), lambda i, j: (0, ep_tile(j))),
                  pl.BlockSpec((width, tn), lambda i, j: (0, ep_tile(j) + nj)),
                  pl.BlockSpec((1, tn), lambda i, j: (0, ep_tile(j))),
                  pl.BlockSpec((1, tn), lambda i, j: (0, ep_tile(j) + nj))],
        out_specs=pl.BlockSpec((tm, tn), lambda i, j: (i, ep_tile(j))),
        out_shape=jax.ShapeDtypeStruct((m, d_ff), BF16),
        scratch_shapes=[pltpu.VMEM((tm + FFN_HALO, k), BF16),
                        u_scratch, u_scratch, u_scratch, u_scratch],
        compiler_params=_params("parallel", "arbitrary"), name="ffn_in",
    )(h, h, w, w, dw_w, dw_w, dw_b.reshape(1, n2), dw_b.reshape(1, n2))


def ffn_out(a, w):
    m, k = a.shape
    n = w.shape[1]
    tm, tn = _tile(m, 512), _tile(n, 256)
    return pl.pallas_call(
        _matmul_kernel, grid=(m // tm, n // tn),
        in_specs=[pl.BlockSpec((tm, k), lambda i, j: (i, 0)),
                  pl.BlockSpec((k, tn), lambda i, j: (0, j))],
        out_specs=pl.BlockSpec((tm, tn), lambda i, j: (i, j)),
        out_shape=jax.ShapeDtypeStruct((m, n), BF16),
        compiler_params=_params("parallel", "arbitrary"), name="ffn_out",
    )(a, w)


def _pow2_slopes(n):
    start = 2.0 ** (-8.0 / n)
    return [start ** (i + 1) for i in range(n)]


def _alibi_slopes(n):
    if math.log2(n).is_integer():
        return _pow2_slopes(n)
    c = 2 ** math.floor(math.log2(n))
    return _pow2_slopes(c) + _alibi_slopes(2 * c)[0::2][: n - c]


def kernel(x, mix_pre_g, mix_post_g, ffn_pre_g, ffn_post_g, w_in, att_sinks, conv_dw_w, conv_dw_b,
           conv_ln_g, conv_ln_b, conv_pw_w, w_out, ffn_w_in, ffn_dw_w, ffn_dw_b, ffn_w_out):
    b, s, d = x.shape
    depth = w_in.shape[0]
    att_heads = att_sinks.shape[1]
    conv_ch = conv_dw_b.shape[1]
    att_q_cols = att_heads * HEAD_DIM
    att_kv_cols = att_q_cols // GQA_RATIO
    ret_heads = (w_in.shape[2] - att_q_cols - 2 * att_kv_cols - 2 * conv_ch) // (4 * HEAD_DIM)
    conv_col0 = att_q_cols + 2 * att_kv_cols
    ret_col0 = conv_col0 + 2 * conv_ch
    assert s % WINDOW == 0 and s % RET_CHUNK == 0

    slopes = jnp.asarray(_alibi_slopes(att_heads), dtype=F32)
    log_gamma = jnp.log1p(-jnp.exp2(-5.0 - jnp.arange(ret_heads, dtype=F32)))

    xf = x.reshape(b * s, d)
    y = None
    for l in range(depth):
        if l == 0:
            h = prenorm(xf, mix_pre_g[l])
        else:
            xf, h = resid_prenorm(xf, y, ffn_post_g[l - 1], mix_pre_g[l])
        proj = in_proj(h, w_in[l].astype(BF16))
        att = attention(proj, att_sinks[l], slopes, b, s, att_heads)
        conv = conformer(proj, conv_dw_w[l], conv_dw_b[l], conv_ln_g[l], conv_ln_b[l],
                         conv_pw_w[l].astype(BF16), b, s, conv_col0)
        ret = retention(proj, log_gamma, b, s, ret_heads, ret_col0)
        y = mix_out(att, conv, ret, w_out[l].astype(BF16))
        xf, h = resid_prenorm(xf, y, mix_post_g[l], ffn_pre_g[l])
        a = ffn_in(h, ffn_w_in[l].astype(BF16), ffn_dw_w[l], ffn_dw_b[l], s)
        y = ffn_out(a, ffn_w_out[l].astype(BF16))
    xf = resid(xf, y, ffn_post_g[depth - 1])
    return xf.reshape(b, s, d)
```

```python
import functools
import math

import jax
import jax.numpy as jnp
from jax import lax
from jax.experimental import pallas as pl
from jax.experimental.pallas import tpu as pltpu

HEAD_DIM = 128
WINDOW = 128
RET_CHUNK = 128
NORM_EPS = 1e-6
NEG_INF = -1e30
GQA_RATIO = 3

F32 = jnp.float32
BF16 = jnp.bfloat16

V7X_VMEM_BYTES = 64 * 1024 * 1024
VMEM_LIMIT_BYTES = 56 * 1024 * 1024


def _params(*semantics, flags=None):
    return pltpu.CompilerParams(dimension_semantics=semantics,
                                vmem_limit_bytes=VMEM_LIMIT_BYTES, flags=flags)


def _tile(n, pref):
    t = min(n, pref)
    while n % t:
        t //= 2
    return t


def _rms(v):
    return v * lax.rsqrt(jnp.mean(v * v, axis=-1, keepdims=True) + NORM_EPS)


def _prenorm_kernel(x_ref, g_ref, h_ref):
    h_ref[...] = (_rms(x_ref[...]) * g_ref[...]).astype(h_ref.dtype)


def _resid_prenorm_kernel(x_ref, y_ref, gpost_ref, gpre_ref, xo_ref, h_ref):
    x = x_ref[...] + _rms(y_ref[...].astype(F32)) * gpost_ref[...]
    xo_ref[...] = x
    h_ref[...] = (_rms(x) * gpre_ref[...]).astype(h_ref.dtype)


def _resid_kernel(x_ref, y_ref, gpost_ref, xo_ref):
    xo_ref[...] = x_ref[...] + _rms(y_ref[...].astype(F32)) * gpost_ref[...]


def _row_spec(tr, d):
    return pl.BlockSpec((tr, d), lambda i: (i, 0))


def _gain_spec(d):
    return pl.BlockSpec((1, d), lambda i: (0, 0))


def prenorm(x, g):
    m, d = x.shape
    tr = _tile(m, 256)
    return pl.pallas_call(
        _prenorm_kernel, grid=(m // tr,),
        in_specs=[_row_spec(tr, d), _gain_spec(d)],
        out_specs=_row_spec(tr, d),
        out_shape=jax.ShapeDtypeStruct((m, d), BF16),
        compiler_params=_params("parallel"), name="prenorm",
    )(x, g.reshape(1, d))


def resid_prenorm(x, y, gpost, gpre):
    m, d = x.shape
    tr = _tile(m, 256)
    return pl.pallas_call(
        _resid_prenorm_kernel, grid=(m // tr,),
        in_specs=[_row_spec(tr, d), _row_spec(tr, d), _gain_spec(d), _gain_spec(d)],
        out_specs=[_row_spec(tr, d), _row_spec(tr, d)],
        out_shape=[jax.ShapeDtypeStruct((m, d), F32), jax.ShapeDtypeStruct((m, d), BF16)],
        compiler_params=_params("parallel"), name="resid_prenorm",
    )(x, y, gpost.reshape(1, d), gpre.reshape(1, d))


def resid(x, y, gpost):
    m, d = x.shape
    tr = _tile(m, 256)
    return pl.pallas_call(
        _resid_kernel, grid=(m // tr,),
        in_specs=[_row_spec(tr, d), _row_spec(tr, d), _gain_spec(d)],
        out_specs=_row_spec(tr, d),
        out_shape=jax.ShapeDtypeStruct((m, d), F32),
        compiler_params=_params("parallel"), name="resid",
    )(x, y, gpost.reshape(1, d))


def _matmul_kernel(a_ref, b_ref, o_ref):
    o_ref[...] = jnp.dot(a_ref[...], b_ref[...], preferred_element_type=F32).astype(o_ref.dtype)


def _cast_weight_once(w_ref, wb_ref):
    @pl.when(pl.program_id(1) == 0)
    def _():
        wb_ref[...] = w_ref[...].astype(BF16)


def _in_proj_kernel(a_ref, w_ref, o_ref, wb_ref):
    _cast_weight_once(w_ref, wb_ref)
    o_ref[...] = jnp.dot(a_ref[...], wb_ref[...], preferred_element_type=F32).astype(o_ref.dtype)


def in_proj(h, w_all, layer):
    m, k = h.shape
    n = w_all.shape[2]
    tm, tn = _tile(m, 1024), _tile(n, 512)
    return pl.pallas_call(
        _in_proj_kernel, grid=(n // tn, m // tm),
        in_specs=[pl.BlockSpec((tm, k), lambda j, i: (i, 0)),
                  pl.BlockSpec((None, k, tn), lambda j, i: (layer, 0, j))],
        out_specs=pl.BlockSpec((tm, tn), lambda j, i: (i, j)),
        out_shape=jax.ShapeDtypeStruct((m, n), BF16),
        scratch_shapes=[pltpu.VMEM((k, tn), BF16)],
        compiler_params=_params("parallel", "arbitrary"), name="in_proj",
    )(h, w_all)


def _attn_kernel(sink_ref, slope_ref, q_ref, kc_ref, vc_ref, kp_ref, vp_ref, o_ref, *, n_kv):
    n = pl.program_id(1)
    blk = WINDOW
    qi = lax.broadcasted_iota(jnp.int32, (blk, 2 * blk), 0)
    kj = lax.broadcasted_iota(jnp.int32, (blk, 2 * blk), 1)
    dist = qi + blk - kj
    valid = (dist >= 0) & (dist < WINDOW) & ((kj >= blk) | (n > 0))
    distf = dist.astype(F32)
    scale = HEAD_DIM ** -0.5
    for hk in range(n_kv):
        ks = slice(hk * HEAD_DIM, (hk + 1) * HEAD_DIM)
        k = jnp.concatenate([kp_ref[:, ks], kc_ref[:, ks]], axis=0)
        v = jnp.concatenate([vp_ref[:, ks], vc_ref[:, ks]], axis=0)
        for g in range(GQA_RATIO):
            h = hk * GQA_RATIO + g
            hs = slice(h * HEAD_DIM, (h + 1) * HEAD_DIM)
            s = lax.dot_general(q_ref[:, hs], k, (((1,), (1,)), ((), ())),
                                preferred_element_type=F32)
            s = s * scale - slope_ref[h] * distf
            s = jnp.where(valid, s, NEG_INF)
            sink = sink_ref[h]
            mx = jnp.maximum(jnp.max(s, axis=-1, keepdims=True), sink)
            p = jnp.exp(s - mx)
            denom = jnp.sum(p, axis=-1, keepdims=True) + jnp.exp(sink - mx)
            out = jnp.dot(p.astype(BF16), v, preferred_element_type=F32) / denom
            o_ref[:, hs] = out.astype(o_ref.dtype)


def attention(proj, sinks, slopes, batch, seq, n_heads):
    m = proj.shape[0]
    n_kv = n_heads // GQA_RATIO
    qw, kw = n_heads * HEAD_DIM, n_kv * HEAD_DIM
    nb = seq // WINDOW
    k_blk, v_blk = qw // kw, qw // kw + 1
    cur = lambda b, n: b * nb + n
    prev = lambda b, n: b * nb + jnp.maximum(n - 1, 0)
    smem = pl.BlockSpec(memory_space=pltpu.SMEM)
    return pl.pallas_call(
        functools.partial(_attn_kernel, n_kv=n_kv), grid=(batch, nb),
        in_specs=[smem, smem,
                  pl.BlockSpec((WINDOW, qw), lambda b, n: (cur(b, n), 0)),
                  pl.BlockSpec((WINDOW, kw), lambda b, n: (cur(b, n), k_blk)),
                  pl.BlockSpec((WINDOW, kw), lambda b, n: (cur(b, n), v_blk)),
                  pl.BlockSpec((WINDOW, kw), lambda b, n: (prev(b, n), k_blk)),
                  pl.BlockSpec((WINDOW, kw), lambda b, n: (prev(b, n), v_blk))],
        out_specs=pl.BlockSpec((WINDOW, qw), lambda b, n: (cur(b, n), 0)),
        out_shape=jax.ShapeDtypeStruct((m, qw), BF16),
        compiler_params=_params("parallel", "parallel"), name="swa_attention",
    )(sinks, slopes, proj, proj, proj, proj, proj)


CONV_HALO = 32
SUBLANES = 8


def _conformer_kernel(a0_ref, a1_ref, g0_ref, g1_ref, dww_ref, dwb_ref, lng_ref, lnb_ref,
                      pw_ref, o_ref, ysh_ref, cbuf_ref, *, ts, width, rows_per_chunk):
    half = a0_ref.shape[1]
    ch = 2 * half
    buf_rows = ts + CONV_HALO

    @pl.when(pl.program_id(1) == 0)
    def _():
        ysh_ref[0, 0:CONV_HALO, :] = jnp.zeros((CONV_HALO, ch), F32)

    for c, (a_ref, g_ref) in enumerate(((a0_ref, g0_ref), (a1_ref, g1_ref))):
        a = a_ref[...].astype(F32)
        g = g_ref[...].astype(F32)
        ysh_ref[0, CONV_HALO:, c * half:(c + 1) * half] = a * jax.nn.sigmoid(g)

    for b in range(1, SUBLANES):
        ysh_ref[b, SUBLANES:, :] = ysh_ref[0, SUBLANES - b:buf_rows - b, :]

    lanes = 128

    def conv_chunk(r, carry):
        r0 = pl.multiple_of(r * rows_per_chunk, rows_per_chunk)
        for c0 in range(0, ch, lanes):
            acc = jnp.broadcast_to(dwb_ref[:, c0:c0 + lanes], (rows_per_chunk, lanes))
            for delay in range(width):
                a, b = divmod(delay, SUBLANES)
                k = width - 1 - delay
                acc = acc + dww_ref[k:k + 1, c0:c0 + lanes] * ysh_ref[
                    b, pl.ds(r0 + CONV_HALO - SUBLANES * a, rows_per_chunk), c0:c0 + lanes]
            cbuf_ref[pl.ds(r0, rows_per_chunk), c0:c0 + lanes] = acc
        return carry

    lax.fori_loop(0, ts // rows_per_chunk, conv_chunk, 0)
    ysh_ref[0, 0:CONV_HALO, :] = ysh_ref[0, ts:buf_rows, :]

    y = cbuf_ref[...]
    mu = jnp.mean(y, axis=-1, keepdims=True)
    yc = y - mu
    yn = yc * lax.rsqrt(jnp.mean(yc * yc, axis=-1, keepdims=True) + NORM_EPS)
    yn = yn * lng_ref[...] + lnb_ref[...]
    act = yn * jax.nn.sigmoid(yn)
    o_ref[...] = jnp.dot(act.astype(BF16), pw_ref[...], preferred_element_type=F32).astype(o_ref.dtype)


def conformer(proj, dw_w, dw_b, ln_g, ln_b, pw_w, batch, seq, col0):
    m = proj.shape[0]
    width, ch = dw_w.shape
    half = ch // 2
    assert width - 1 <= CONV_HALO and col0 % half == 0
    ts = _tile(seq, 512)
    ns = seq // ts
    cb = col0 // half
    row = lambda b, s: b * ns + s
    col_spec = lambda j: pl.BlockSpec((ts, half), lambda b, s: (row(b, s), cb + j))
    full = lambda shape: pl.BlockSpec(shape, lambda b, s: (0, 0))
    return pl.pallas_call(
        functools.partial(_conformer_kernel, ts=ts, width=width, rows_per_chunk=64),
        grid=(batch, ns),
        in_specs=[col_spec(0), col_spec(1), col_spec(2), col_spec(3),
                  full((width, ch)), full((1, ch)), full((1, ch)), full((1, ch)), full((ch, ch))],
        out_specs=pl.BlockSpec((ts, ch), lambda b, s: (row(b, s), 0)),
        out_shape=jax.ShapeDtypeStruct((m, ch), BF16),
        scratch_shapes=[pltpu.VMEM((SUBLANES, ts + CONV_HALO, ch), F32), pltpu.VMEM((ts, ch), F32)],
        compiler_params=_params("parallel", "arbitrary"), name="conformer_conv",
    )(proj, proj, proj, proj, dw_w, dw_b.reshape(1, ch), ln_g.reshape(1, ch), ln_b.reshape(1, ch), pw_w)


def _retention_kernel(lg_ref, q_ref, k_ref, v_ref, g_ref, o_ref, state_ref, *, n_heads):
    c = RET_CHUNK

    @pl.when(pl.program_id(1) == 0)
    def _():
        state_ref[...] = jnp.zeros(state_ref.shape, F32)

    row = lax.broadcasted_iota(jnp.int32, (c, c), 0)
    col = lax.broadcasted_iota(jnp.int32, (c, c), 1)
    rel = (row - col).astype(F32)
    causal = row >= col
    pos = lax.broadcasted_iota(jnp.int32, (c, 1), 0).astype(F32)
    chunk_len = jnp.full((1, 1), float(c), F32)
    scale = HEAD_DIM ** -0.5
    for h in range(n_heads):
        hs = slice(h * HEAD_DIM, (h + 1) * HEAD_DIM)
        lg = lg_ref[h]
        q, k, v = q_ref[:, hs], k_ref[:, hs], v_ref[:, hs]
        intra_decay = jnp.where(causal, jnp.exp(lg * jnp.maximum(rel, 0.0)), 0.0)
        s = lax.dot_general(q, k, (((1,), (1,)), ((), ())), preferred_element_type=F32)
        s = s * (scale * intra_decay)
        intra = jnp.dot(s.astype(BF16), v, preferred_element_type=F32)
        state = state_ref[h]
        q_decay = jnp.exp(lg * (pos + 1.0))
        cross = q_decay * jnp.dot(q, state.astype(BF16), preferred_element_type=F32)
        k_decay = jnp.exp(lg * (c - 1.0 - pos)) * scale
        kd = (k.astype(F32) * k_decay).astype(BF16)
        kv = lax.dot_general(kd, v, (((0,), (0,)), ((), ())), preferred_element_type=F32)
        state_ref[h] = state * jnp.exp(lg * chunk_len) + kv
        r = intra + cross
        mu = jnp.mean(r, axis=-1, keepdims=True)
        rc = r - mu
        rn = rc * lax.rsqrt(jnp.mean(rc * rc, axis=-1, keepdims=True) + NORM_EPS)
        gate = g_ref[:, hs].astype(F32)
        o_ref[:, hs] = (rn * (gate * jax.nn.sigmoid(gate))).astype(o_ref.dtype)


def retention(proj, log_gamma, batch, seq, n_heads, col0):
    m = proj.shape[0]
    w = n_heads * HEAD_DIM
    assert col0 % w == 0
    cb = col0 // w
    nc = seq // RET_CHUNK
    row = lambda b, n: b * nc + n
    spec = lambda j: pl.BlockSpec((RET_CHUNK, w), lambda b, n: (row(b, n), cb + j))
    return pl.pallas_call(
        functools.partial(_retention_kernel, n_heads=n_heads), grid=(batch, nc),
        in_specs=[pl.BlockSpec(memory_space=pltpu.SMEM), spec(0), spec(1), spec(2), spec(3)],
        out_specs=pl.BlockSpec((RET_CHUNK, w), lambda b, n: (row(b, n), 0)),
        out_shape=jax.ShapeDtypeStruct((m, w), BF16),
        scratch_shapes=[pltpu.VMEM((n_heads, HEAD_DIM, HEAD_DIM), F32)],
        compiler_params=_params("parallel", "arbitrary"), name="retention",
    )(log_gamma, proj, proj, proj, proj)


def _mix_out_kernel(att_ref, conv_ref, ret_ref, w_ref, o_ref, wb_ref):
    _cast_weight_once(w_ref, wb_ref)
    ka, kc = att_ref.shape[1], conv_ref.shape[1]
    acc = jnp.dot(att_ref[...], wb_ref[0:ka, :], preferred_element_type=F32)
    acc += jnp.dot(conv_ref[...], wb_ref[ka:ka + kc, :], preferred_element_type=F32)
    acc += jnp.dot(ret_ref[...], wb_ref[ka + kc:, :], preferred_element_type=F32)
    o_ref[...] = acc.astype(o_ref.dtype)


def mix_out(att, conv, ret, w_all, layer):
    m = att.shape[0]
    _, k, n = w_all.shape
    tm, tn = _tile(m, 1024), _tile(n, 512)
    lhs = lambda a: pl.BlockSpec((tm, a.shape[1]), lambda j, i: (i, 0))
    return pl.pallas_call(
        _mix_out_kernel, grid=(n // tn, m // tm),
        in_specs=[lhs(att), lhs(conv), lhs(ret),
                  pl.BlockSpec((None, k, tn), lambda j, i: (layer, 0, j))],
        out_specs=pl.BlockSpec((tm, tn), lambda j, i: (i, j)),
        out_shape=jax.ShapeDtypeStruct((m, n), BF16),
        scratch_shapes=[pltpu.VMEM((k, tn), BF16)],
        compiler_params=_params("parallel", "arbitrary"), name="mix_out",
    )(att, conv, ret, w_all)


FFN_CARRY = 8


def _gelu_tanh(x):
    return 0.5 * x * (1.0 + jnp.tanh(math.sqrt(2.0 / math.pi) * (x + 0.044715 * (x * x * x))))


def _ffn_in_kernel(h_ref, wg_ref, wv_ref, dwg_ref, dwv_ref, bg_ref, bv_ref, o_ref,
                   wgb_ref, wvb_ref, ug_ref, uv_ref, *, tm, width, tiles_per_seq):
    i = pl.program_id(1)
    _cast_weight_once(wg_ref, wgb_ref)
    _cast_weight_once(wv_ref, wvb_ref)

    seq_start = (i % tiles_per_seq) == 0

    @pl.when(seq_start)
    def _():
        ug_ref[0:FFN_CARRY, :] = jnp.zeros((FFN_CARRY, ug_ref.shape[1]), F32)
        uv_ref[0:FFN_CARRY, :] = jnp.zeros((FFN_CARRY, uv_ref.shape[1]), F32)

    @pl.when(jnp.logical_not(seq_start))
    def _():
        ug_ref[0:FFN_CARRY, :] = ug_ref[tm:tm + FFN_CARRY, :]
        uv_ref[0:FFN_CARRY, :] = uv_ref[tm:tm + FFN_CARRY, :]

    h = h_ref[...]
    ug_ref[FFN_CARRY:, :] = jnp.dot(h, wgb_ref[...], preferred_element_type=F32)
    uv_ref[FFN_CARRY:, :] = jnp.dot(h, wvb_ref[...], preferred_element_type=F32)

    def conv(u_ref, dw_ref, b_ref):
        acc = b_ref[...]
        for k in range(width):
            acc = acc + dw_ref[k:k + 1, :] * u_ref[pl.ds(FFN_CARRY - (width - 1) + k, tm), :]
        return acc

    gate = conv(ug_ref, dwg_ref, bg_ref)
    val = conv(uv_ref, dwv_ref, bv_ref)
    o_ref[...] = (_gelu_tanh(gate) * val).astype(o_ref.dtype)


def ffn_in(h, w_all, layer, dw_w, dw_b, seq):
    m, k = h.shape
    width, n2 = dw_w.shape
    d_ff = n2 // 2
    assert width - 1 <= FFN_CARRY
    tm, tn = _tile(seq, 1024), _tile(d_ff, 256)
    nj = d_ff // tn
    u_scratch = pltpu.VMEM((tm + FFN_CARRY, tn), F32)
    return pl.pallas_call(
        functools.partial(_ffn_in_kernel, tm=tm, width=width, tiles_per_seq=seq // tm),
        grid=(nj, m // tm),
        in_specs=[pl.BlockSpec((tm, k), lambda j, i: (i, 0)),
                  pl.BlockSpec((None, k, tn), lambda j, i: (layer, 0, j)),
                  pl.BlockSpec((None, k, tn), lambda j, i: (layer, 0, j + nj)),
                  pl.BlockSpec((width, tn), lambda j, i: (0, j)),
                  pl.BlockSpec((width, tn), lambda j, i: (0, j + nj)),
                  pl.BlockSpec((1, tn), lambda j, i: (0, j)),
                  pl.BlockSpec((1, tn), lambda j, i: (0, j + nj))],
        out_specs=pl.BlockSpec((tm, tn), lambda j, i: (i, j)),
        out_shape=jax.ShapeDtypeStruct((m, d_ff), BF16),
        scratch_shapes=[pltpu.VMEM((k, tn), BF16), pltpu.VMEM((k, tn), BF16), u_scratch, u_scratch],
        compiler_params=_params("parallel", "arbitrary"), name="ffn_in",
    )(h, w_all, w_all, dw_w, dw_w, dw_b.reshape(1, n2), dw_b.reshape(1, n2))


def ffn_out(a, w):
    m, k = a.shape
    n = w.shape[1]
    tm, tn = _tile(m, 512), _tile(n, 512)
    return pl.pallas_call(
        _matmul_kernel, grid=(m // tm, n // tn),
        in_specs=[pl.BlockSpec((tm, k), lambda i, j: (i, 0)),
                  pl.BlockSpec((k, tn), lambda i, j: (0, j))],
        out_specs=pl.BlockSpec((tm, tn), lambda i, j: (i, j)),
        out_shape=jax.ShapeDtypeStruct((m, n), BF16),
        compiler_params=_params("parallel", "arbitrary"), name="ffn_out",
    )(a, w)


def _pow2_slopes(n):
    start = 2.0 ** (-8.0 / n)
    return [start ** (i + 1) for i in range(n)]


def _alibi_slopes(n):
    if math.log2(n).is_integer():
        return _pow2_slopes(n)
    c = 2 ** math.floor(math.log2(n))
    return _pow2_slopes(c) + _alibi_slopes(2 * c)[0::2][: n - c]


def kernel(x, mix_pre_g, mix_post_g, ffn_pre_g, ffn_post_g, w_in, att_sinks, conv_dw_w, conv_dw_b,
           conv_ln_g, conv_ln_b, conv_pw_w, w_out, ffn_w_in, ffn_dw_w, ffn_dw_b, ffn_w_out):
    b, s, d = x.shape
    depth = w_in.shape[0]
    att_heads = att_sinks.shape[1]
    conv_ch = conv_dw_b.shape[1]
    att_q_cols = att_heads * HEAD_DIM
    att_kv_cols = att_q_cols // GQA_RATIO
    ret_heads = (w_in.shape[2] - att_q_cols - 2 * att_kv_cols - 2 * conv_ch) // (4 * HEAD_DIM)
    conv_col0 = att_q_cols + 2 * att_kv_cols
    ret_col0 = conv_col0 + 2 * conv_ch
    assert s % WINDOW == 0 and s % RET_CHUNK == 0

    slopes = jnp.asarray(_alibi_slopes(att_heads), dtype=F32)
    log_gamma = jnp.log1p(-jnp.exp2(-5.0 - jnp.arange(ret_heads, dtype=F32)))

    xf = x.reshape(b * s, d)
    y = None
    for l in range(depth):
        if l == 0:
            h = prenorm(xf, mix_pre_g[l])
        else:
            xf, h = resid_prenorm(xf, y, ffn_post_g[l - 1], mix_pre_g[l])
        proj = in_proj(h, w_in, l)
        att = attention(proj, att_sinks[l], slopes, b, s, att_heads)
        conv = conformer(proj, conv_dw_w[l], conv_dw_b[l], conv_ln_g[l], conv_ln_b[l],
                         conv_pw_w[l].astype(BF16), b, s, conv_col0)
        ret = retention(proj, log_gamma, b, s, ret_heads, ret_col0)
        y = mix_out(att, conv, ret, w_out, l)
        xf, h = resid_prenorm(xf, y, mix_post_g[l], ffn_pre_g[l])
        a = ffn_in(h, ffn_w_in, l, ffn_dw_w[l], ffn_dw_b[l], s)
        y = ffn_out(a, ffn_w_out[l].astype(BF16))
    xf = resid(xf, y, ffn_post_g[depth - 1])
    return xf.reshape(b, s, d)
```

```python
import functools
import math

import jax
import jax.numpy as jnp
from jax import lax
from jax.experimental import pallas as pl
from jax.experimental.pallas import tpu as pltpu

HEAD_DIM = 128
WINDOW = 128
RET_CHUNK = 128
NORM_EPS = 1e-6
NEG_INF = -1e30
GQA_RATIO = 3

F32 = jnp.float32
BF16 = jnp.bfloat16

V7X_VMEM_BYTES = 64 * 1024 * 1024
VMEM_LIMIT_BYTES = 56 * 1024 * 1024


def _params(*semantics, flags=None):
    return pltpu.CompilerParams(dimension_semantics=semantics,
                                vmem_limit_bytes=VMEM_LIMIT_BYTES, flags=flags)


def _tile(n, pref):
    t = min(n, pref)
    while n % t:
        t //= 2
    return t


def _rms(v):
    return v * lax.rsqrt(jnp.mean(v * v, axis=-1, keepdims=True) + NORM_EPS)


def _prenorm_kernel(x_ref, g_ref, h_ref):
    h_ref[...] = (_rms(x_ref[...]) * g_ref[...]).astype(h_ref.dtype)


def _resid_prenorm_kernel(x_ref, y_ref, gpost_ref, gpre_ref, xo_ref, h_ref):
    x = x_ref[...] + _rms(y_ref[...].astype(F32)) * gpost_ref[...]
    xo_ref[...] = x
    h_ref[...] = (_rms(x) * gpre_ref[...]).astype(h_ref.dtype)


def _resid_kernel(x_ref, y_ref, gpost_ref, xo_ref):
    xo_ref[...] = x_ref[...] + _rms(y_ref[...].astype(F32)) * gpost_ref[...]


def _row_spec(tr, d):
    return pl.BlockSpec((tr, d), lambda i: (i, 0))


def _gain_spec(d):
    return pl.BlockSpec((1, d), lambda i: (0, 0))


def prenorm(x, g):
    m, d = x.shape
    tr = _tile(m, 256)
    return pl.pallas_call(
        _prenorm_kernel, grid=(m // tr,),
        in_specs=[_row_spec(tr, d), _gain_spec(d)],
        out_specs=_row_spec(tr, d),
        out_shape=jax.ShapeDtypeStruct((m, d), BF16),
        compiler_params=_params("parallel"), name="prenorm",
    )(x, g.reshape(1, d))


def resid_prenorm(x, y, gpost, gpre):
    m, d = x.shape
    tr = _tile(m, 256)
    return pl.pallas_call(
        _resid_prenorm_kernel, grid=(m // tr,),
        in_specs=[_row_spec(tr, d), _row_spec(tr, d), _gain_spec(d), _gain_spec(d)],
        out_specs=[_row_spec(tr, d), _row_spec(tr, d)],
        out_shape=[jax.ShapeDtypeStruct((m, d), F32), jax.ShapeDtypeStruct((m, d), BF16)],
        compiler_params=_params("parallel"), name="resid_prenorm",
    )(x, y, gpost.reshape(1, d), gpre.reshape(1, d))


def resid(x, y, gpost):
    m, d = x.shape
    tr = _tile(m, 256)
    return pl.pallas_call(
        _resid_kernel, grid=(m // tr,),
        in_specs=[_row_spec(tr, d), _row_spec(tr, d), _gain_spec(d)],
        out_specs=_row_spec(tr, d),
        out_shape=jax.ShapeDtypeStruct((m, d), F32),
        compiler_params=_params("parallel"), name="resid",
    )(x, y, gpost.reshape(1, d))


def _matmul_kernel(a_ref, b_ref, o_ref):
    o_ref[...] = jnp.dot(a_ref[...], b_ref[...], preferred_element_type=F32).astype(o_ref.dtype)


def _cast_weight_once(w_ref, wb_ref):
    @pl.when(pl.program_id(1) == 0)
    def _():
        wb_ref[...] = w_ref[...].astype(BF16)


def _in_proj_kernel(a_ref, w_ref, o_ref, wb_ref):
    _cast_weight_once(w_ref, wb_ref)
    o_ref[...] = jnp.dot(a_ref[...], wb_ref[...], preferred_element_type=F32).astype(o_ref.dtype)


def in_proj(h, w_all, layer):
    m, k = h.shape
    n = w_all.shape[2]
    tm, tn = _tile(m, 1024), _tile(n, 512)
    return pl.pallas_call(
        _in_proj_kernel, grid=(n // tn, m // tm),
        in_specs=[pl.BlockSpec((tm, k), lambda j, i: (i, 0)),
                  pl.BlockSpec((None, k, tn), lambda j, i: (layer, 0, j))],
        out_specs=pl.BlockSpec((tm, tn), lambda j, i: (i, j)),
        out_shape=jax.ShapeDtypeStruct((m, n), BF16),
        scratch_shapes=[pltpu.VMEM((k, tn), BF16)],
        compiler_params=_params("parallel", "arbitrary"), name="in_proj",
    )(h, w_all)


def _attn_kernel(sink_ref, slope_ref, q_ref, kc_ref, vc_ref, kp_ref, vp_ref, o_ref, *, n_kv):
    n = pl.program_id(1)
    blk = WINDOW
    qi = lax.broadcasted_iota(jnp.int32, (blk, 2 * blk), 0)
    kj = lax.broadcasted_iota(jnp.int32, (blk, 2 * blk), 1)
    dist = qi + blk - kj
    valid = (dist >= 0) & (dist < WINDOW) & ((kj >= blk) | (n > 0))
    distf = dist.astype(F32)
    scale = HEAD_DIM ** -0.5
    for hk in range(n_kv):
        ks = slice(hk * HEAD_DIM, (hk + 1) * HEAD_DIM)
        k = jnp.concatenate([kp_ref[:, ks], kc_ref[:, ks]], axis=0)
        v = jnp.concatenate([vp_ref[:, ks], vc_ref[:, ks]], axis=0)
        for g in range(GQA_RATIO):
            h = hk * GQA_RATIO + g
            hs = slice(h * HEAD_DIM, (h + 1) * HEAD_DIM)
            s = lax.dot_general(q_ref[:, hs], k, (((1,), (1,)), ((), ())),
                                preferred_element_type=F32)
            s = s * scale - slope_ref[h] * distf
            s = jnp.where(valid, s, NEG_INF)
            sink = sink_ref[h]
            mx = jnp.maximum(jnp.max(s, axis=-1, keepdims=True), sink)
            p = jnp.exp(s - mx)
            denom = jnp.sum(p, axis=-1, keepdims=True) + jnp.exp(sink - mx)
            out = jnp.dot(p.astype(BF16), v, preferred_element_type=F32) / denom
            o_ref[:, hs] = out.astype(o_ref.dtype)


def attention(proj, sinks, slopes, batch, seq, n_heads):
    m = proj.shape[0]
    n_kv = n_heads // GQA_RATIO
    qw, kw = n_heads * HEAD_DIM, n_kv * HEAD_DIM
    nb = seq // WINDOW
    k_blk, v_blk = qw // kw, qw // kw + 1
    cur = lambda b, n: b * nb + n
    prev = lambda b, n: b * nb + jnp.maximum(n - 1, 0)
    smem = pl.BlockSpec(memory_space=pltpu.SMEM)
    return pl.pallas_call(
        functools.partial(_attn_kernel, n_kv=n_kv), grid=(batch, nb),
        in_specs=[smem, smem,
                  pl.BlockSpec((WINDOW, qw), lambda b, n: (cur(b, n), 0)),
                  pl.BlockSpec((WINDOW, kw), lambda b, n: (cur(b, n), k_blk)),
                  pl.BlockSpec((WINDOW, kw), lambda b, n: (cur(b, n), v_blk)),
                  pl.BlockSpec((WINDOW, kw), lambda b, n: (prev(b, n), k_blk)),
                  pl.BlockSpec((WINDOW, kw), lambda b, n: (prev(b, n), v_blk))],
        out_specs=pl.BlockSpec((WINDOW, qw), lambda b, n: (cur(b, n), 0)),
        out_shape=jax.ShapeDtypeStruct((m, qw), BF16),
        compiler_params=_params("parallel", "parallel"), name="swa_attention",
    )(sinks, slopes, proj, proj, proj, proj, proj)


CONV_HALO = 32
SUBLANES = 8


def _conformer_kernel(a0_ref, a1_ref, g0_ref, g1_ref, dww_ref, dwb_ref, lng_ref, lnb_ref,
                      pw_ref, o_ref, ysh_ref, cbuf_ref, *, ts, width, rows_per_chunk):
    half = a0_ref.shape[1]
    ch = 2 * half
    buf_rows = ts + CONV_HALO

    @pl.when(pl.program_id(1) == 0)
    def _():
        ysh_ref[0, 0:CONV_HALO, :] = jnp.zeros((CONV_HALO, ch), F32)

    for c, (a_ref, g_ref) in enumerate(((a0_ref, g0_ref), (a1_ref, g1_ref))):
        a = a_ref[...].astype(F32)
        g = g_ref[...].astype(F32)
        ysh_ref[0, CONV_HALO:, c * half:(c + 1) * half] = a * jax.nn.sigmoid(g)

    for b in range(1, SUBLANES):
        ysh_ref[b, SUBLANES:, :] = ysh_ref[0, SUBLANES - b:buf_rows - b, :]

    lanes = 128

    n_aligned = -(-width // SUBLANES)
    back = SUBLANES * (n_aligned - 1)

    def conv_chunk(r, carry):
        r0 = pl.multiple_of(r * rows_per_chunk, rows_per_chunk)
        for c0 in range(0, ch, lanes):
            acc = jnp.broadcast_to(dwb_ref[:, c0:c0 + lanes], (rows_per_chunk, lanes))
            for b in range(SUBLANES):
                ext = ysh_ref[b, pl.ds(r0 + CONV_HALO - back, rows_per_chunk + back), c0:c0 + lanes]
                for a in range(n_aligned):
                    delay = SUBLANES * a + b
                    if delay < width:
                        k = width - 1 - delay
                        lo = back - SUBLANES * a
                        acc = acc + dww_ref[k:k + 1, c0:c0 + lanes] * ext[lo:lo + rows_per_chunk]
            cbuf_ref[pl.ds(r0, rows_per_chunk), c0:c0 + lanes] = acc
        return carry

    lax.fori_loop(0, ts // rows_per_chunk, conv_chunk, 0)
    ysh_ref[0, 0:CONV_HALO, :] = ysh_ref[0, ts:buf_rows, :]

    y = cbuf_ref[...]
    mu = jnp.mean(y, axis=-1, keepdims=True)
    yc = y - mu
    yn = yc * lax.rsqrt(jnp.mean(yc * yc, axis=-1, keepdims=True) + NORM_EPS)
    yn = yn * lng_ref[...] + lnb_ref[...]
    act = yn * jax.nn.sigmoid(yn)
    o_ref[...] = jnp.dot(act.astype(BF16), pw_ref[...], preferred_element_type=F32).astype(o_ref.dtype)


def conformer(proj, dw_w, dw_b, ln_g, ln_b, pw_all, layer, batch, seq, col0):
    m = proj.shape[0]
    width, ch = dw_w.shape
    half = ch // 2
    assert width - 1 <= CONV_HALO and col0 % half == 0
    ts = _tile(seq, 512)
    ns = seq // ts
    cb = col0 // half
    row = lambda b, s: b * ns + s
    col_spec = lambda j: pl.BlockSpec((ts, half), lambda b, s: (row(b, s), cb + j))
    full = lambda shape: pl.BlockSpec(shape, lambda b, s: (0, 0))
    return pl.pallas_call(
        functools.partial(_conformer_kernel, ts=ts, width=width, rows_per_chunk=64),
        grid=(batch, ns),
        in_specs=[col_spec(0), col_spec(1), col_spec(2), col_spec(3),
                  full((width, ch)), full((1, ch)), full((1, ch)), full((1, ch)),
                  pl.BlockSpec((None, ch, ch), lambda b, s: (layer, 0, 0))],
        out_specs=pl.BlockSpec((ts, ch), lambda b, s: (row(b, s), 0)),
        out_shape=jax.ShapeDtypeStruct((m, ch), BF16),
        scratch_shapes=[pltpu.VMEM((SUBLANES, ts + CONV_HALO, ch), F32), pltpu.VMEM((ts, ch), F32)],
        compiler_params=_params("parallel", "arbitrary"), name="conformer_conv",
    )(proj, proj, proj, proj, dw_w, dw_b.reshape(1, ch), ln_g.reshape(1, ch), ln_b.reshape(1, ch), pw_all)


def _retention_kernel(lg_ref, q_ref, k_ref, v_ref, g_ref, o_ref, state_ref, *, n_heads):
    c = RET_CHUNK

    @pl.when(pl.program_id(1) == 0)
    def _():
        state_ref[...] = jnp.zeros(state_ref.shape, F32)

    row = lax.broadcasted_iota(jnp.int32, (c, c), 0)
    col = lax.broadcasted_iota(jnp.int32, (c, c), 1)
    rel = (row - col).astype(F32)
    causal = row >= col
    pos = lax.broadcasted_iota(jnp.int32, (c, 1), 0).astype(F32)
    chunk_len = jnp.full((1, 1), float(c), F32)
    scale = HEAD_DIM ** -0.5
    for h in range(n_heads):
        hs = slice(h * HEAD_DIM, (h + 1) * HEAD_DIM)
        lg = lg_ref[h]
        q, k, v = q_ref[:, hs], k_ref[:, hs], v_ref[:, hs]
        intra_decay = jnp.where(causal, jnp.exp(lg * jnp.maximum(rel, 0.0)), 0.0)
        s = lax.dot_general(q, k, (((1,), (1,)), ((), ())), preferred_element_type=F32)
        s = s * (scale * intra_decay)
        intra = jnp.dot(s.astype(BF16), v, preferred_element_type=F32)
        state = state_ref[h]
        q_decay = jnp.exp(lg * (pos + 1.0))
        cross = q_decay * jnp.dot(q, state.astype(BF16), preferred_element_type=F32)
        k_decay = jnp.exp(lg * (c - 1.0 - pos)) * scale
        kd = (k.astype(F32) * k_decay).astype(BF16)
        kv = lax.dot_general(kd, v, (((0,), (0,)), ((), ())), preferred_element_type=F32)
        state_ref[h] = state * jnp.exp(lg * chunk_len) + kv
        r = intra + cross
        mu = jnp.mean(r, axis=-1, keepdims=True)
        rc = r - mu
        rn = rc * lax.rsqrt(jnp.mean(rc * rc, axis=-1, keepdims=True) + NORM_EPS)
        gate = g_ref[:, hs].astype(F32)
        o_ref[:, hs] = (rn * (gate * jax.nn.sigmoid(gate))).astype(o_ref.dtype)


def retention(proj, log_gamma, batch, seq, n_heads, col0):
    m = proj.shape[0]
    w = n_heads * HEAD_DIM
    assert col0 % w == 0
    cb = col0 // w
    nc = seq // RET_CHUNK
    row = lambda b, n: b * nc + n
    spec = lambda j: pl.BlockSpec((RET_CHUNK, w), lambda b, n: (row(b, n), cb + j))
    return pl.pallas_call(
        functools.partial(_retention_kernel, n_heads=n_heads), grid=(batch, nc),
        in_specs=[pl.BlockSpec(memory_space=pltpu.SMEM), spec(0), spec(1), spec(2), spec(3)],
        out_specs=pl.BlockSpec((RET_CHUNK, w), lambda b, n: (row(b, n), 0)),
        out_shape=jax.ShapeDtypeStruct((m, w), BF16),
        scratch_shapes=[pltpu.VMEM((n_heads, HEAD_DIM, HEAD_DIM), F32)],
        compiler_params=_params("parallel", "arbitrary"), name="retention",
    )(log_gamma, proj, proj, proj, proj)


def _mix_out_kernel(att_ref, conv_ref, ret_ref, w_ref, o_ref, wb_ref):
    _cast_weight_once(w_ref, wb_ref)
    ka, kc = att_ref.shape[1], conv_ref.shape[1]
    acc = jnp.dot(att_ref[...], wb_ref[0:ka, :], preferred_element_type=F32)
    acc += jnp.dot(conv_ref[...], wb_ref[ka:ka + kc, :], preferred_element_type=F32)
    acc += jnp.dot(ret_ref[...], wb_ref[ka + kc:, :], preferred_element_type=F32)
    o_ref[...] = acc.astype(o_ref.dtype)


def mix_out(att, conv, ret, w_all, layer):
    m = att.shape[0]
    _, k, n = w_all.shape
    tm, tn = _tile(m, 1024), _tile(n, 512)
    lhs = lambda a: pl.BlockSpec((tm, a.shape[1]), lambda j, i: (i, 0))
    return pl.pallas_call(
        _mix_out_kernel, grid=(n // tn, m // tm),
        in_specs=[lhs(att), lhs(conv), lhs(ret),
                  pl.BlockSpec((None, k, tn), lambda j, i: (layer, 0, j))],
        out_specs=pl.BlockSpec((tm, tn), lambda j, i: (i, j)),
        out_shape=jax.ShapeDtypeStruct((m, n), BF16),
        scratch_shapes=[pltpu.VMEM((k, tn), BF16)],
        compiler_params=_params("parallel", "arbitrary"), name="mix_out",
    )(att, conv, ret, w_all)


FFN_CARRY = 8


def _gelu_tanh(x):
    return 0.5 * x * (1.0 + jnp.tanh(math.sqrt(2.0 / math.pi) * (x + 0.044715 * (x * x * x))))


def _ffn_in_kernel(h_ref, wg_ref, wv_ref, dwg_ref, dwv_ref, bg_ref, bv_ref, o_ref,
                   wgb_ref, wvb_ref, ug_ref, uv_ref, *, tm, width, tiles_per_seq):
    i = pl.program_id(1)
    _cast_weight_once(wg_ref, wgb_ref)
    _cast_weight_once(wv_ref, wvb_ref)

    seq_start = (i % tiles_per_seq) == 0

    @pl.when(seq_start)
    def _():
        ug_ref[0:FFN_CARRY, :] = jnp.zeros((FFN_CARRY, ug_ref.shape[1]), F32)
        uv_ref[0:FFN_CARRY, :] = jnp.zeros((FFN_CARRY, uv_ref.shape[1]), F32)

    @pl.when(jnp.logical_not(seq_start))
    def _():
        ug_ref[0:FFN_CARRY, :] = ug_ref[tm:tm + FFN_CARRY, :]
        uv_ref[0:FFN_CARRY, :] = uv_ref[tm:tm + FFN_CARRY, :]

    h = h_ref[...]
    ug_ref[FFN_CARRY:, :] = jnp.dot(h, wgb_ref[...], preferred_element_type=F32)
    uv_ref[FFN_CARRY:, :] = jnp.dot(h, wvb_ref[...], preferred_element_type=F32)

    def conv(u_ref, dw_ref, b_ref):
        u = u_ref[...]
        acc = dw_ref[0:1, :] * u
        for k in range(1, width):
            acc = dw_ref[k:k + 1, :] * u + pltpu.roll(acc, 1, axis=0)
        return acc[FFN_CARRY:, :] + b_ref[...]

    gate = conv(ug_ref, dwg_ref, bg_ref)
    val = conv(uv_ref, dwv_ref, bv_ref)
    o_ref[...] = (_gelu_tanh(gate) * val).astype(o_ref.dtype)


def ffn_in(h, w_all, layer, dw_w, dw_b, seq):
    m, k = h.shape
    width, n2 = dw_w.shape
    d_ff = n2 // 2
    assert width - 1 <= FFN_CARRY
    tm, tn = _tile(seq, 1024), _tile(d_ff, 256)
    nj = d_ff // tn
    u_scratch = pltpu.VMEM((tm + FFN_CARRY, tn), F32)
    return pl.pallas_call(
        functools.partial(_ffn_in_kernel, tm=tm, width=width, tiles_per_seq=seq // tm),
        grid=(nj, m // tm),
        in_specs=[pl.BlockSpec((tm, k), lambda j, i: (i, 0)),
                  pl.BlockSpec((None, k, tn), lambda j, i: (layer, 0, j)),
                  pl.BlockSpec((None, k, tn), lambda j, i: (layer, 0, j + nj)),
                  pl.BlockSpec((width, tn), lambda j, i: (0, j)),
                  pl.BlockSpec((width, tn), lambda j, i: (0, j + nj)),
                  pl.BlockSpec((1, tn), lambda j, i: (0, j)),
                  pl.BlockSpec((1, tn), lambda j, i: (0, j + nj))],
        out_specs=pl.BlockSpec((tm, tn), lambda j, i: (i, j)),
        out_shape=jax.ShapeDtypeStruct((m, d_ff), BF16),
        scratch_shapes=[pltpu.VMEM((k, tn), BF16), pltpu.VMEM((k, tn), BF16), u_scratch, u_scratch],
        compiler_params=_params("parallel", "arbitrary"), name="ffn_in",
    )(h, w_all, w_all, dw_w, dw_w, dw_b.reshape(1, n2), dw_b.reshape(1, n2))


def ffn_out(a, w_all, layer):
    m, k = a.shape
    n = w_all.shape[2]
    tm, tn = _tile(m, 512), _tile(n, 512)
    return pl.pallas_call(
        _matmul_kernel, grid=(m // tm, n // tn),
        in_specs=[pl.BlockSpec((tm, k), lambda i, j: (i, 0)),
                  pl.BlockSpec((None, k, tn), lambda i, j: (layer, 0, j))],
        out_specs=pl.BlockSpec((tm, tn), lambda i, j: (i, j)),
        out_shape=jax.ShapeDtypeStruct((m, n), BF16),
        compiler_params=_params("parallel", "arbitrary"), name="ffn_out",
    )(a, w_all)


CAST_BLOCK_BYTES = 8 * 1024 * 1024


def _cast_kernel(w_ref, o_ref):
    o_ref[...] = w_ref[...].astype(o_ref.dtype)


def cast_bf16(w_all):
    depth, k, n = w_all.shape
    tk = _tile(k, max(16, CAST_BLOCK_BYTES // (4 * n)))
    spec = pl.BlockSpec((None, tk, n), lambda l, i: (l, i, 0))
    return pl.pallas_call(
        _cast_kernel, grid=(depth, k // tk), in_specs=[spec], out_specs=spec,
        out_shape=jax.ShapeDtypeStruct(w_all.shape, BF16),
        compiler_params=_params("parallel", "parallel"), name="cast_bf16",
    )(w_all)


def _pow2_slopes(n):
    start = 2.0 ** (-8.0 / n)
    return [start ** (i + 1) for i in range(n)]


def _alibi_slopes(n):
    if math.log2(n).is_integer():
        return _pow2_slopes(n)
    c = 2 ** math.floor(math.log2(n))
    return _pow2_slopes(c) + _alibi_slopes(2 * c)[0::2][: n - c]


def kernel(x, mix_pre_g, mix_post_g, ffn_pre_g, ffn_post_g, w_in, att_sinks, conv_dw_w, conv_dw_b,
           conv_ln_g, conv_ln_b, conv_pw_w, w_out, ffn_w_in, ffn_dw_w, ffn_dw_b, ffn_w_out):
    b, s, d = x.shape
    depth = w_in.shape[0]
    att_heads = att_sinks.shape[1]
    conv_ch = conv_dw_b.shape[1]
    att_q_cols = att_heads * HEAD_DIM
    att_kv_cols = att_q_cols // GQA_RATIO
    ret_heads = (w_in.shape[2] - att_q_cols - 2 * att_kv_cols - 2 * conv_ch) // (4 * HEAD_DIM)
    conv_col0 = att_q_cols + 2 * att_kv_cols
    ret_col0 = conv_col0 + 2 * conv_ch
    assert s % WINDOW == 0 and s % RET_CHUNK == 0

    slopes = jnp.asarray(_alibi_slopes(att_heads), dtype=F32)
    log_gamma = jnp.log1p(-jnp.exp2(-5.0 - jnp.arange(ret_heads, dtype=F32)))

    conv_pw_bf16 = cast_bf16(conv_pw_w)
    ffn_w_out_bf16 = cast_bf16(ffn_w_out)

    xf = x.reshape(b * s, d)
    y = None
    for l in range(depth):
        if l == 0:
            h = prenorm(xf, mix_pre_g[l])
        else:
            xf, h = resid_prenorm(xf, y, ffn_post_g[l - 1], mix_pre_g[l])
        proj = in_proj(h, w_in, l)
        att = attention(proj, att_sinks[l], slopes, b, s, att_heads)
        conv = conformer(proj, conv_dw_w[l], conv_dw_b[l], conv_ln_g[l], conv_ln_b[l],
                         conv_pw_bf16, l, b, s, conv_col0)
        ret = retention(proj, log_gamma, b, s, ret_heads, ret_col0)
        y = mix_out(att, conv, ret, w_out, l)
        xf, h = resid_prenorm(xf, y, mix_post_g[l], ffn_pre_g[l])
        a = ffn_in(h, ffn_w_in, l, ffn_dw_w[l], ffn_dw_b[l], s)
        y = ffn_out(a, ffn_w_out_bf16, l)
    xf = resid(xf, y, ffn_post_g[depth - 1])
    return xf.reshape(b, s, d)
```

```python
import functools
import math

import jax
import jax.numpy as jnp
from jax import lax
from jax.experimental import pallas as pl
from jax.experimental.pallas import tpu as pltpu

HEAD_DIM = 128
WINDOW = 128
RET_CHUNK = 128
NORM_EPS = 1e-6
NEG_INF = -1e30
GQA_RATIO = 3

F32 = jnp.float32
BF16 = jnp.bfloat16

V7X_VMEM_BYTES = 64 * 1024 * 1024
VMEM_LIMIT_BYTES = 56 * 1024 * 1024


def _params(*semantics, flags=None):
    return pltpu.CompilerParams(dimension_semantics=semantics,
                                vmem_limit_bytes=VMEM_LIMIT_BYTES, flags=flags)


def _tile(n, pref):
    t = min(n, pref)
    while n % t:
        t //= 2
    return t


def _rms(v):
    return v * lax.rsqrt(jnp.mean(v * v, axis=-1, keepdims=True) + NORM_EPS)


def _prenorm_kernel(x_ref, g_ref, h_ref):
    h_ref[...] = (_rms(x_ref[...]) * g_ref[...]).astype(h_ref.dtype)


def _resid_prenorm_kernel(x_ref, y_ref, gpost_ref, gpre_ref, xo_ref, h_ref):
    x = x_ref[...] + _rms(y_ref[...].astype(F32)) * gpost_ref[...]
    xo_ref[...] = x
    h_ref[...] = (_rms(x) * gpre_ref[...]).astype(h_ref.dtype)


def _resid_kernel(x_ref, y_ref, gpost_ref, xo_ref):
    xo_ref[...] = x_ref[...] + _rms(y_ref[...].astype(F32)) * gpost_ref[...]


def _row_spec(tr, d):
    return pl.BlockSpec((tr, d), lambda i: (i, 0))


def _gain_spec(d):
    return pl.BlockSpec((1, d), lambda i: (0, 0))


def prenorm(x, g):
    m, d = x.shape
    tr = _tile(m, 256)
    return pl.pallas_call(
        _prenorm_kernel, grid=(m // tr,),
        in_specs=[_row_spec(tr, d), _gain_spec(d)],
        out_specs=_row_spec(tr, d),
        out_shape=jax.ShapeDtypeStruct((m, d), BF16),
        compiler_params=_params("parallel"), name="prenorm",
    )(x, g.reshape(1, d))


def resid_prenorm(x, y, gpost, gpre):
    m, d = x.shape
    tr = _tile(m, 256)
    return pl.pallas_call(
        _resid_prenorm_kernel, grid=(m // tr,),
        in_specs=[_row_spec(tr, d), _row_spec(tr, d), _gain_spec(d), _gain_spec(d)],
        out_specs=[_row_spec(tr, d), _row_spec(tr, d)],
        out_shape=[jax.ShapeDtypeStruct((m, d), F32), jax.ShapeDtypeStruct((m, d), BF16)],
        compiler_params=_params("parallel"), name="resid_prenorm",
    )(x, y, gpost.reshape(1, d), gpre.reshape(1, d))


def resid(x, y, gpost):
    m, d = x.shape
    tr = _tile(m, 256)
    return pl.pallas_call(
        _resid_kernel, grid=(m // tr,),
        in_specs=[_row_spec(tr, d), _row_spec(tr, d), _gain_spec(d)],
        out_specs=_row_spec(tr, d),
        out_shape=jax.ShapeDtypeStruct((m, d), F32),
        compiler_params=_params("parallel"), name="resid",
    )(x, y, gpost.reshape(1, d))


def _matmul_kernel(a_ref, b_ref, o_ref):
    o_ref[...] = jnp.dot(a_ref[...], b_ref[...], preferred_element_type=F32).astype(o_ref.dtype)


def _cast_weight_once(w_ref, wb_ref):
    @pl.when(pl.program_id(1) == 0)
    def _():
        wb_ref[...] = w_ref[...].astype(BF16)


def _in_proj_kernel(a_ref, w_ref, o_ref, wb_ref):
    _cast_weight_once(w_ref, wb_ref)
    o_ref[...] = jnp.dot(a_ref[...], wb_ref[...], preferred_element_type=F32).astype(o_ref.dtype)


def in_proj(h, w_all, layer):
    m, k = h.shape
    n = w_all.shape[2]
    tm, tn = _tile(m, 1024), _tile(n, 512)
    return pl.pallas_call(
        _in_proj_kernel, grid=(n // tn, m // tm),
        in_specs=[pl.BlockSpec((tm, k), lambda j, i: (i, 0)),
                  pl.BlockSpec((None, k, tn), lambda j, i: (layer, 0, j))],
        out_specs=pl.BlockSpec((tm, tn), lambda j, i: (i, j)),
        out_shape=jax.ShapeDtypeStruct((m, n), BF16),
        scratch_shapes=[pltpu.VMEM((k, tn), BF16)],
        compiler_params=_params("parallel", "arbitrary"), name="in_proj",
    )(h, w_all)


def _attn_kernel(sink_ref, slope_ref, q_ref, kc_ref, vc_ref, kp_ref, vp_ref, o_ref, *, n_kv):
    n = pl.program_id(1)
    blk = WINDOW
    qi = lax.broadcasted_iota(jnp.int32, (blk, 2 * blk), 0)
    kj = lax.broadcasted_iota(jnp.int32, (blk, 2 * blk), 1)
    dist = qi + blk - kj
    valid = (dist >= 0) & (dist < WINDOW) & ((kj >= blk) | (n > 0))
    distf = dist.astype(F32)
    scale = HEAD_DIM ** -0.5
    for hk in range(n_kv):
        ks = slice(hk * HEAD_DIM, (hk + 1) * HEAD_DIM)
        k = jnp.concatenate([kp_ref[:, ks], kc_ref[:, ks]], axis=0)
        v = jnp.concatenate([vp_ref[:, ks], vc_ref[:, ks]], axis=0)
        for g in range(GQA_RATIO):
            h = hk * GQA_RATIO + g
            hs = slice(h * HEAD_DIM, (h + 1) * HEAD_DIM)
            s = lax.dot_general(q_ref[:, hs], k, (((1,), (1,)), ((), ())),
                                preferred_element_type=F32)
            s = s * scale - slope_ref[h] * distf
            s = jnp.where(valid, s, NEG_INF)
            sink = sink_ref[h]
            mx = jnp.maximum(jnp.max(s, axis=-1, keepdims=True), sink)
            p = jnp.exp(s - mx)
            denom = jnp.sum(p, axis=-1, keepdims=True) + jnp.exp(sink - mx)
            out = jnp.dot(p.astype(BF16), v, preferred_element_type=F32) / denom
            o_ref[:, hs] = out.astype(o_ref.dtype)


def attention(proj, sinks, slopes, batch, seq, n_heads):
    m = proj.shape[0]
    n_kv = n_heads // GQA_RATIO
    qw, kw = n_heads * HEAD_DIM, n_kv * HEAD_DIM
    nb = seq // WINDOW
    k_blk, v_blk = qw // kw, qw // kw + 1
    cur = lambda b, n: b * nb + n
    prev = lambda b, n: b * nb + jnp.maximum(n - 1, 0)
    smem = pl.BlockSpec(memory_space=pltpu.SMEM)
    return pl.pallas_call(
        functools.partial(_attn_kernel, n_kv=n_kv), grid=(batch, nb),
        in_specs=[smem, smem,
                  pl.BlockSpec((WINDOW, qw), lambda b, n: (cur(b, n), 0)),
                  pl.BlockSpec((WINDOW, kw), lambda b, n: (cur(b, n), k_blk)),
                  pl.BlockSpec((WINDOW, kw), lambda b, n: (cur(b, n), v_blk)),
                  pl.BlockSpec((WINDOW, kw), lambda b, n: (prev(b, n), k_blk)),
                  pl.BlockSpec((WINDOW, kw), lambda b, n: (prev(b, n), v_blk))],
        out_specs=pl.BlockSpec((WINDOW, qw), lambda b, n: (cur(b, n), 0)),
        out_shape=jax.ShapeDtypeStruct((m, qw), BF16),
        compiler_params=_params("parallel", "parallel"), name="swa_attention",
    )(sinks, slopes, proj, proj, proj, proj, proj)


CONV_HALO = 32
SUBLANES = 8


def _conformer_kernel(a0_ref, a1_ref, g0_ref, g1_ref, dww_ref, dwb_ref, lng_ref, lnb_ref,
                      pw_ref, o_ref, ysh_ref, cbuf_ref, *, ts, width, rows_per_chunk):
    half = a0_ref.shape[1]
    ch = 2 * half
    buf_rows = ts + CONV_HALO

    @pl.when(pl.program_id(1) == 0)
    def _():
        ysh_ref[0, 0:CONV_HALO, :] = jnp.zeros((CONV_HALO, ch), F32)

    for c, (a_ref, g_ref) in enumerate(((a0_ref, g0_ref), (a1_ref, g1_ref))):
        a = a_ref[...].astype(F32)
        g = g_ref[...].astype(F32)
        ysh_ref[0, CONV_HALO:, c * half:(c + 1) * half] = a * jax.nn.sigmoid(g)

    y0 = ysh_ref[0]
    for b in range(1, SUBLANES):
        ysh_ref[b, SUBLANES:, :] = pltpu.roll(y0, b, axis=0)[SUBLANES:, :]

    lanes = 128

    n_aligned = -(-width // SUBLANES)
    back = SUBLANES * (n_aligned - 1)

    def conv_chunk(r, carry):
        r0 = pl.multiple_of(r * rows_per_chunk, rows_per_chunk)
        for c0 in range(0, ch, lanes):
            acc = jnp.broadcast_to(dwb_ref[:, c0:c0 + lanes], (rows_per_chunk, lanes))
            for b in range(SUBLANES):
                ext = ysh_ref[b, pl.ds(r0 + CONV_HALO - back, rows_per_chunk + back), c0:c0 + lanes]
                for a in range(n_aligned):
                    delay = SUBLANES * a + b
                    if delay < width:
                        k = width - 1 - delay
                        lo = back - SUBLANES * a
                        acc = acc + dww_ref[k:k + 1, c0:c0 + lanes] * ext[lo:lo + rows_per_chunk]
            cbuf_ref[pl.ds(r0, rows_per_chunk), c0:c0 + lanes] = acc
        return carry

    lax.fori_loop(0, ts // rows_per_chunk, conv_chunk, 0)
    ysh_ref[0, 0:CONV_HALO, :] = ysh_ref[0, ts:buf_rows, :]

    y = cbuf_ref[...]
    mu = jnp.mean(y, axis=-1, keepdims=True)
    yc = y - mu
    yn = yc * lax.rsqrt(jnp.mean(yc * yc, axis=-1, keepdims=True) + NORM_EPS)
    yn = yn * lng_ref[...] + lnb_ref[...]
    act = yn * jax.nn.sigmoid(yn)
    o_ref[...] = jnp.dot(act.astype(BF16), pw_ref[...], preferred_element_type=F32).astype(o_ref.dtype)


def conformer(proj, dw_w, dw_b, ln_g, ln_b, pw_all, layer, batch, seq, col0):
    m = proj.shape[0]
    width, ch = dw_w.shape
    half = ch // 2
    assert width - 1 <= CONV_HALO and col0 % half == 0
    ts = _tile(seq, 512)
    ns = seq // ts
    cb = col0 // half
    row = lambda b, s: b * ns + s
    col_spec = lambda j: pl.BlockSpec((ts, half), lambda b, s: (row(b, s), cb + j))
    full = lambda shape: pl.BlockSpec(shape, lambda b, s: (0, 0))
    return pl.pallas_call(
        functools.partial(_conformer_kernel, ts=ts, width=width, rows_per_chunk=64),
        grid=(batch, ns),
        in_specs=[col_spec(0), col_spec(1), col_spec(2), col_spec(3),
                  full((width, ch)), full((1, ch)), full((1, ch)), full((1, ch)),
                  pl.BlockSpec((None, ch, ch), lambda b, s: (layer, 0, 0))],
        out_specs=pl.BlockSpec((ts, ch), lambda b, s: (row(b, s), 0)),
        out_shape=jax.ShapeDtypeStruct((m, ch), BF16),
        scratch_shapes=[pltpu.VMEM((SUBLANES, ts + CONV_HALO, ch), F32), pltpu.VMEM((ts, ch), F32)],
        compiler_params=_params("parallel", "arbitrary"), name="conformer_conv",
    )(proj, proj, proj, proj, dw_w, dw_b.reshape(1, ch), ln_g.reshape(1, ch), ln_b.reshape(1, ch), pw_all)


def _retention_kernel(lg_ref, q_ref, k_ref, v_ref, g_ref, o_ref, state_ref, decay_ref, *, n_heads):
    c = RET_CHUNK
    scale = HEAD_DIM ** -0.5

    @pl.when(pl.program_id(1) == 0)
    def _():
        state_ref[...] = jnp.zeros(state_ref.shape, F32)
        row = lax.broadcasted_iota(jnp.int32, (c, c), 0)
        col = lax.broadcasted_iota(jnp.int32, (c, c), 1)
        rel = (row - col).astype(F32)
        for h in range(n_heads):
            decay_ref[h] = scale * jnp.where(row >= col, jnp.exp(lg_ref[h] * jnp.maximum(rel, 0.0)), 0.0)

    pos = lax.broadcasted_iota(jnp.int32, (c, 1), 0).astype(F32)
    chunk_len = jnp.full((1, 1), float(c), F32)
    for h in range(n_heads):
        hs = slice(h * HEAD_DIM, (h + 1) * HEAD_DIM)
        lg = lg_ref[h]
        q, k, v = q_ref[:, hs], k_ref[:, hs], v_ref[:, hs]
        s = lax.dot_general(q, k, (((1,), (1,)), ((), ())), preferred_element_type=F32)
        s = s * decay_ref[h]
        intra = jnp.dot(s.astype(BF16), v, preferred_element_type=F32)
        state = state_ref[h]
        q_decay = jnp.exp(lg * (pos + 1.0))
        cross = q_decay * jnp.dot(q, state.astype(BF16), preferred_element_type=F32)
        k_decay = jnp.exp(lg * (c - 1.0 - pos)) * scale
        kd = (k.astype(F32) * k_decay).astype(BF16)
        kv = lax.dot_general(kd, v, (((0,), (0,)), ((), ())), preferred_element_type=F32)
        state_ref[h] = state * jnp.exp(lg * chunk_len) + kv
        r = intra + cross
        mu = jnp.mean(r, axis=-1, keepdims=True)
        rc = r - mu
        rn = rc * lax.rsqrt(jnp.mean(rc * rc, axis=-1, keepdims=True) + NORM_EPS)
        gate = g_ref[:, hs].astype(F32)
        o_ref[:, hs] = (rn * (gate * jax.nn.sigmoid(gate))).astype(o_ref.dtype)


def retention(proj, log_gamma, batch, seq, n_heads, col0):
    m = proj.shape[0]
    w = n_heads * HEAD_DIM
    assert col0 % w == 0
    cb = col0 // w
    nc = seq // RET_CHUNK
    row = lambda b, n: b * nc + n
    spec = lambda j: pl.BlockSpec((RET_CHUNK, w), lambda b, n: (row(b, n), cb + j))
    return pl.pallas_call(
        functools.partial(_retention_kernel, n_heads=n_heads), grid=(batch, nc),
        in_specs=[pl.BlockSpec(memory_space=pltpu.SMEM), spec(0), spec(1), spec(2), spec(3)],
        out_specs=pl.BlockSpec((RET_CHUNK, w), lambda b, n: (row(b, n), 0)),
        out_shape=jax.ShapeDtypeStruct((m, w), BF16),
        scratch_shapes=[pltpu.VMEM((n_heads, HEAD_DIM, HEAD_DIM), F32),
                        pltpu.VMEM((n_heads, RET_CHUNK, RET_CHUNK), F32)],
        compiler_params=_params("parallel", "arbitrary"), name="retention",
    )(log_gamma, proj, proj, proj, proj)


def _mix_out_kernel(att_ref, conv_ref, ret_ref, w_ref, o_ref, wb_ref):
    _cast_weight_once(w_ref, wb_ref)
    ka, kc = att_ref.shape[1], conv_ref.shape[1]
    acc = jnp.dot(att_ref[...], wb_ref[0:ka, :], preferred_element_type=F32)
    acc += jnp.dot(conv_ref[...], wb_ref[ka:ka + kc, :], preferred_element_type=F32)
    acc += jnp.dot(ret_ref[...], wb_ref[ka + kc:, :], preferred_element_type=F32)
    o_ref[...] = acc.astype(o_ref.dtype)


def mix_out(att, conv, ret, w_all, layer):
    m = att.shape[0]
    _, k, n = w_all.shape
    tm, tn = _tile(m, 1024), _tile(n, 512)
    lhs = lambda a: pl.BlockSpec((tm, a.shape[1]), lambda j, i: (i, 0))
    return pl.pallas_call(
        _mix_out_kernel, grid=(n // tn, m // tm),
        in_specs=[lhs(att), lhs(conv), lhs(ret),
                  pl.BlockSpec((None, k, tn), lambda j, i: (layer, 0, j))],
        out_specs=pl.BlockSpec((tm, tn), lambda j, i: (i, j)),
        out_shape=jax.ShapeDtypeStruct((m, n), BF16),
        scratch_shapes=[pltpu.VMEM((k, tn), BF16)],
        compiler_params=_params("parallel", "arbitrary"), name="mix_out",
    )(att, conv, ret, w_all)


FFN_CARRY = 8


GELU_C1 = math.sqrt(2.0 / math.pi)
GELU_C2 = GELU_C1 * 0.044715


def _gelu_tanh(x):
    g = 0.5 * x
    return g + g * jnp.tanh(x * (GELU_C1 + GELU_C2 * (x * x)))


def _ffn_in_kernel(h_ref, wg_ref, wv_ref, dwg_ref, dwv_ref, bg_ref, bv_ref, o_ref,
                   wgb_ref, wvb_ref, ug_ref, uv_ref, *, tm, width, tiles_per_seq):
    i = pl.program_id(1)
    _cast_weight_once(wg_ref, wgb_ref)
    _cast_weight_once(wv_ref, wvb_ref)

    seq_start = (i % tiles_per_seq) == 0

    @pl.when(seq_start)
    def _():
        ug_ref[0:FFN_CARRY, :] = jnp.zeros((FFN_CARRY, ug_ref.shape[1]), F32)
        uv_ref[0:FFN_CARRY, :] = jnp.zeros((FFN_CARRY, uv_ref.shape[1]), F32)

    @pl.when(jnp.logical_not(seq_start))
    def _():
        ug_ref[0:FFN_CARRY, :] = ug_ref[tm:tm + FFN_CARRY, :]
        uv_ref[0:FFN_CARRY, :] = uv_ref[tm:tm + FFN_CARRY, :]

    h = h_ref[...]
    ug_ref[FFN_CARRY:, :] = jnp.dot(h, wgb_ref[...], preferred_element_type=F32)
    uv_ref[FFN_CARRY:, :] = jnp.dot(h, wvb_ref[...], preferred_element_type=F32)

    def conv(u_ref, dw_ref, b_ref):
        u = u_ref[...]
        acc = dw_ref[0:1, :] * u
        for k in range(1, width):
            acc = dw_ref[k:k + 1, :] * u + pltpu.roll(acc, 1, axis=0)
        return acc[FFN_CARRY:, :] + b_ref[...]

    gate = conv(ug_ref, dwg_ref, bg_ref)
    val = conv(uv_ref, dwv_ref, bv_ref)
    o_ref[...] = (_gelu_tanh(gate) * val).astype(o_ref.dtype)


def ffn_in(h, w_all, layer, dw_w, dw_b, seq):
    m, k = h.shape
    width, n2 = dw_w.shape
    d_ff = n2 // 2
    assert width - 1 <= FFN_CARRY
    tm, tn = _tile(seq, 1024), _tile(d_ff, 256)
    nj = d_ff // tn
    u_scratch = pltpu.VMEM((tm + FFN_CARRY, tn), F32)
    return pl.pallas_call(
        functools.partial(_ffn_in_kernel, tm=tm, width=width, tiles_per_seq=seq // tm),
        grid=(nj, m // tm),
        in_specs=[pl.BlockSpec((tm, k), lambda j, i: (i, 0)),
                  pl.BlockSpec((None, k, tn), lambda j, i: (layer, 0, j)),
                  pl.BlockSpec((None, k, tn), lambda j, i: (layer, 0, j + nj)),
                  pl.BlockSpec((width, tn), lambda j, i: (0, j)),
                  pl.BlockSpec((width, tn), lambda j, i: (0, j + nj)),
                  pl.BlockSpec((1, tn), lambda j, i: (0, j)),
                  pl.BlockSpec((1, tn), lambda j, i: (0, j + nj))],
        out_specs=pl.BlockSpec((tm, tn), lambda j, i: (i, j)),
        out_shape=jax.ShapeDtypeStruct((m, d_ff), BF16),
        scratch_shapes=[pltpu.VMEM((k, tn), BF16), pltpu.VMEM((k, tn), BF16), u_scratch, u_scratch],
        compiler_params=_params("parallel", "arbitrary"), name="ffn_in",
    )(h, w_all, w_all, dw_w, dw_w, dw_b.reshape(1, n2), dw_b.reshape(1, n2))


def ffn_out(a, w_all, layer):
    m, k = a.shape
    n = w_all.shape[2]
    tm, tn = _tile(m, 512), _tile(n, 512)
    return pl.pallas_call(
        _matmul_kernel, grid=(m // tm, n // tn),
        in_specs=[pl.BlockSpec((tm, k), lambda i, j: (i, 0)),
                  pl.BlockSpec((None, k, tn), lambda i, j: (layer, 0, j))],
        out_specs=pl.BlockSpec((tm, tn), lambda i, j: (i, j)),
        out_shape=jax.ShapeDtypeStruct((m, n), BF16),
        compiler_params=_params("parallel", "arbitrary"), name="ffn_out",
    )(a, w_all)


CAST_BLOCK_BYTES = 8 * 1024 * 1024


def _cast_kernel(w_ref, o_ref):
    o_ref[...] = w_ref[...].astype(o_ref.dtype)


def cast_bf16(w_all):
    depth, k, n = w_all.shape
    tk = _tile(k, max(16, CAST_BLOCK_BYTES // (4 * n)))
    spec = pl.BlockSpec((None, tk, n), lambda l, i: (l, i, 0))
    return pl.pallas_call(
        _cast_kernel, grid=(depth, k // tk), in_specs=[spec], out_specs=spec,
        out_shape=jax.ShapeDtypeStruct(w_all.shape, BF16),
        compiler_params=_params("parallel", "parallel"), name="cast_bf16",
    )(w_all)


def _pow2_slopes(n):
    start = 2.0 ** (-8.0 / n)
    return [start ** (i + 1) for i in range(n)]


def _alibi_slopes(n):
    if math.log2(n).is_integer():
        return _pow2_slopes(n)
    c = 2 ** math.floor(math.log2(n))
    return _pow2_slopes(c) + _alibi_slopes(2 * c)[0::2][: n - c]


def kernel(x, mix_pre_g, mix_post_g, ffn_pre_g, ffn_post_g, w_in, att_sinks, conv_dw_w, conv_dw_b,
           conv_ln_g, conv_ln_b, conv_pw_w, w_out, ffn_w_in, ffn_dw_w, ffn_dw_b, ffn_w_out):
    b, s, d = x.shape
    depth = w_in.shape[0]
    att_heads = att_sinks.shape[1]
    conv_ch = conv_dw_b.shape[1]
    att_q_cols = att_heads * HEAD_DIM
    att_kv_cols = att_q_cols // GQA_RATIO
    ret_heads = (w_in.shape[2] - att_q_cols - 2 * att_kv_cols - 2 * conv_ch) // (4 * HEAD_DIM)
    conv_col0 = att_q_cols + 2 * att_kv_cols
    ret_col0 = conv_col0 + 2 * conv_ch
    assert s % WINDOW == 0 and s % RET_CHUNK == 0

    slopes = jnp.asarray(_alibi_slopes(att_heads), dtype=F32)
    log_gamma = jnp.log1p(-jnp.exp2(-5.0 - jnp.arange(ret_heads, dtype=F32)))

    conv_pw_bf16 = cast_bf16(conv_pw_w)
    ffn_w_out_bf16 = cast_bf16(ffn_w_out)

    xf = x.reshape(b * s, d)
    y = None
    for l in range(depth):
        if l == 0:
            h = prenorm(xf, mix_pre_g[l])
        else:
            xf, h = resid_prenorm(xf, y, ffn_post_g[l - 1], mix_pre_g[l])
        proj = in_proj(h, w_in, l)
        att = attention(proj, att_sinks[l], slopes, b, s, att_heads)
        conv = conformer(proj, conv_dw_w[l], conv_dw_b[l], conv_ln_g[l], conv_ln_b[l],
                         conv_pw_bf16, l, b, s, conv_col0)
        ret = retention(proj, log_gamma, b, s, ret_heads, ret_col0)
        y = mix_out(att, conv, ret, w_out, l)
        xf, h = resid_prenorm(xf, y, mix_post_g[l], ffn_pre_g[l])
        a = ffn_in(h, ffn_w_in, l, ffn_dw_w[l], ffn_dw_b[l], s)
        y = ffn_out(a, ffn_w_out_bf16, l)
    xf = resid(xf, y, ffn_post_g[depth - 1])
    return xf.reshape(b, s, d)
```

```python
import functools
import math

import jax
import jax.numpy as jnp
from jax import lax
from jax.experimental import pallas as pl
from jax.experimental.pallas import tpu as pltpu

HEAD_DIM = 128
WINDOW = 128
RET_CHUNK = 128
NORM_EPS = 1e-6
NEG_INF = -1e30
GQA_RATIO = 3

F32 = jnp.float32
BF16 = jnp.bfloat16

V7X_VMEM_BYTES = 64 * 1024 * 1024
VMEM_LIMIT_BYTES = 56 * 1024 * 1024


def _params(*semantics, flags=None):
    return pltpu.CompilerParams(dimension_semantics=semantics,
                                vmem_limit_bytes=VMEM_LIMIT_BYTES, flags=flags)


def _tile(n, pref):
    t = min(n, pref)
    while n % t:
        t //= 2
    return t


def _rms(v):
    return v * lax.rsqrt(jnp.mean(v * v, axis=-1, keepdims=True) + NORM_EPS)


def _prenorm_kernel(x_ref, g_ref, h_ref):
    h_ref[...] = (_rms(x_ref[...]) * g_ref[...]).astype(h_ref.dtype)


def _resid_prenorm_kernel(x_ref, y_ref, gpost_ref, gpre_ref, xo_ref, h_ref):
    x = x_ref[...] + _rms(y_ref[...].astype(F32)) * gpost_ref[...]
    xo_ref[...] = x
    h_ref[...] = (_rms(x) * gpre_ref[...]).astype(h_ref.dtype)


def _resid_kernel(x_ref, y_ref, gpost_ref, xo_ref):
    xo_ref[...] = x_ref[...] + _rms(y_ref[...].astype(F32)) * gpost_ref[...]


def _row_spec(tr, d):
    return pl.BlockSpec((tr, d), lambda i: (i, 0))


def _gain_spec(d):
    return pl.BlockSpec((1, d), lambda i: (0, 0))


def prenorm(x, g):
    m, d = x.shape
    tr = _tile(m, 256)
    return pl.pallas_call(
        _prenorm_kernel, grid=(m // tr,),
        in_specs=[_row_spec(tr, d), _gain_spec(d)],
        out_specs=_row_spec(tr, d),
        out_shape=jax.ShapeDtypeStruct((m, d), BF16),
        compiler_params=_params("parallel"), name="prenorm",
    )(x, g.reshape(1, d))


def resid_prenorm(x, y, gpost, gpre):
    m, d = x.shape
    tr = _tile(m, 256)
    return pl.pallas_call(
        _resid_prenorm_kernel, grid=(m // tr,),
        in_specs=[_row_spec(tr, d), _row_spec(tr, d), _gain_spec(d), _gain_spec(d)],
        out_specs=[_row_spec(tr, d), _row_spec(tr, d)],
        out_shape=[jax.ShapeDtypeStruct((m, d), F32), jax.ShapeDtypeStruct((m, d), BF16)],
        compiler_params=_params("parallel"), name="resid_prenorm",
    )(x, y, gpost.reshape(1, d), gpre.reshape(1, d))


def resid(x, y, gpost):
    m, d = x.shape
    tr = _tile(m, 256)
    return pl.pallas_call(
        _resid_kernel, grid=(m // tr,),
        in_specs=[_row_spec(tr, d), _row_spec(tr, d), _gain_spec(d)],
        out_specs=_row_spec(tr, d),
        out_shape=jax.ShapeDtypeStruct((m, d), F32),
        compiler_params=_params("parallel"), name="resid",
    )(x, y, gpost.reshape(1, d))


def _matmul_kernel(a_ref, b_ref, o_ref):
    o_ref[...] = jnp.dot(a_ref[...], b_ref[...], preferred_element_type=F32).astype(o_ref.dtype)


def _cast_weight_once(w_ref, wb_ref):
    @pl.when(pl.program_id(1) == 0)
    def _():
        wb_ref[...] = w_ref[...].astype(BF16)


def _in_proj_kernel(a_ref, w_ref, o_ref, wb_ref):
    _cast_weight_once(w_ref, wb_ref)
    o_ref[...] = jnp.dot(a_ref[...], wb_ref[...], preferred_element_type=F32).astype(o_ref.dtype)


def in_proj(h, w_all, layer):
    m, k = h.shape
    n = w_all.shape[2]
    tm, tn = _tile(m, 1024), _tile(n, 512)
    return pl.pallas_call(
        _in_proj_kernel, grid=(n // tn, m // tm),
        in_specs=[pl.BlockSpec((tm, k), lambda j, i: (i, 0)),
                  pl.BlockSpec((None, k, tn), lambda j, i: (layer, 0, j))],
        out_specs=pl.BlockSpec((tm, tn), lambda j, i: (i, j)),
        out_shape=jax.ShapeDtypeStruct((m, n), BF16),
        scratch_shapes=[pltpu.VMEM((k, tn), BF16)],
        compiler_params=_params("parallel", "arbitrary"), name="in_proj",
    )(h, w_all)


LOG2E = math.log2(math.e)


def _attn_kernel(sink_ref, slope_ref, q_ref, kc_ref, vc_ref, kp_ref, vp_ref, o_ref, bias_ref, *, n_kv):
    n = pl.program_id(1)
    blk = WINDOW
    n_heads = n_kv * GQA_RATIO

    @pl.when(n == 0)
    def _():
        qi = lax.broadcasted_iota(jnp.int32, (blk, 2 * blk), 0)
        kj = lax.broadcasted_iota(jnp.int32, (blk, 2 * blk), 1)
        dist = qi + blk - kj
        in_window = (dist >= 0) & (dist < WINDOW)
        distf = dist.astype(F32)
        for first, valid in ((0, in_window & (kj >= blk)), (1, in_window)):
            for h in range(n_heads):
                bias_ref[first, h] = jnp.where(valid, (-LOG2E * slope_ref[h]) * distf, NEG_INF)

    later = jnp.minimum(n, 1)
    scale = HEAD_DIM ** -0.5 * LOG2E
    for hk in range(n_kv):
        ks = slice(hk * HEAD_DIM, (hk + 1) * HEAD_DIM)
        k = jnp.concatenate([kp_ref[:, ks], kc_ref[:, ks]], axis=0)
        v = jnp.concatenate([vp_ref[:, ks], vc_ref[:, ks]], axis=0)
        for g in range(GQA_RATIO):
            h = hk * GQA_RATIO + g
            hs = slice(h * HEAD_DIM, (h + 1) * HEAD_DIM)
            s = lax.dot_general(q_ref[:, hs], k, (((1,), (1,)), ((), ())),
                                preferred_element_type=F32)
            s = s * scale + bias_ref[later, h]
            sink = sink_ref[h] * LOG2E
            mx = jnp.maximum(jnp.max(s, axis=-1, keepdims=True), sink)
            p = jnp.exp2(s - mx)
            denom = jnp.sum(p, axis=-1, keepdims=True) + jnp.exp2(sink - mx)
            out = jnp.dot(p.astype(BF16), v, preferred_element_type=F32) / denom
            o_ref[:, hs] = out.astype(o_ref.dtype)


def attention(proj, sinks, slopes, batch, seq, n_heads):
    m = proj.shape[0]
    n_kv = n_heads // GQA_RATIO
    qw, kw = n_heads * HEAD_DIM, n_kv * HEAD_DIM
    nb = seq // WINDOW
    k_blk, v_blk = qw // kw, qw // kw + 1
    cur = lambda b, n: b * nb + n
    prev = lambda b, n: b * nb + jnp.maximum(n - 1, 0)
    smem = pl.BlockSpec(memory_space=pltpu.SMEM)
    return pl.pallas_call(
        functools.partial(_attn_kernel, n_kv=n_kv), grid=(batch, nb),
        in_specs=[smem, smem,
                  pl.BlockSpec((WINDOW, qw), lambda b, n: (cur(b, n), 0)),
                  pl.BlockSpec((WINDOW, kw), lambda b, n: (cur(b, n), k_blk)),
                  pl.BlockSpec((WINDOW, kw), lambda b, n: (cur(b, n), v_blk)),
                  pl.BlockSpec((WINDOW, kw), lambda b, n: (prev(b, n), k_blk)),
                  pl.BlockSpec((WINDOW, kw), lambda b, n: (prev(b, n), v_blk))],
        out_specs=pl.BlockSpec((WINDOW, qw), lambda b, n: (cur(b, n), 0)),
        out_shape=jax.ShapeDtypeStruct((m, qw), BF16),
        scratch_shapes=[pltpu.VMEM((2, n_heads, WINDOW, 2 * WINDOW), F32)],
        compiler_params=_params("parallel", "arbitrary"), name="swa_attention",
    )(sinks, slopes, proj, proj, proj, proj, proj)


CONV_HALO = 32
SUBLANES = 8


def _conformer_kernel(a0_ref, a1_ref, g0_ref, g1_ref, dww_ref, dwb_ref, lng_ref, lnb_ref,
                      pw_ref, o_ref, ysh_ref, cbuf_ref, *, ts, width, rows_per_chunk):
    half = a0_ref.shape[1]
    ch = 2 * half
    buf_rows = ts + CONV_HALO

    @pl.when(pl.program_id(1) == 0)
    def _():
        ysh_ref[0, 0:CONV_HALO, :] = jnp.zeros((CONV_HALO, ch), F32)

    for c, (a_ref, g_ref) in enumerate(((a0_ref, g0_ref), (a1_ref, g1_ref))):
        a = a_ref[...].astype(F32)
        g = g_ref[...].astype(F32)
        ysh_ref[0, CONV_HALO:, c * half:(c + 1) * half] = a * jax.nn.sigmoid(g)

    y0 = ysh_ref[0]
    for b in range(1, SUBLANES):
        ysh_ref[b, SUBLANES:, :] = pltpu.roll(y0, b, axis=0)[SUBLANES:, :]

    lanes = 128

    n_aligned = -(-width // SUBLANES)
    back = SUBLANES * (n_aligned - 1)

    def conv_chunk(r, carry):
        r0 = pl.multiple_of(r * rows_per_chunk, rows_per_chunk)
        for c0 in range(0, ch, lanes):
            acc = jnp.broadcast_to(dwb_ref[:, c0:c0 + lanes], (rows_per_chunk, lanes))
            for b in range(SUBLANES):
                ext = ysh_ref[b, pl.ds(r0 + CONV_HALO - back, rows_per_chunk + back), c0:c0 + lanes]
                for a in range(n_aligned):
                    delay = SUBLANES * a + b
                    if delay < width:
                        k = width - 1 - delay
                        lo = back - SUBLANES * a
                        acc = acc + dww_ref[k:k + 1, c0:c0 + lanes] * ext[lo:lo + rows_per_chunk]
            cbuf_ref[pl.ds(r0, rows_per_chunk), c0:c0 + lanes] = acc
        return carry

    lax.fori_loop(0, ts // rows_per_chunk, conv_chunk, 0)
    ysh_ref[0, 0:CONV_HALO, :] = ysh_ref[0, ts:buf_rows, :]

    y = cbuf_ref[...]
    mu = jnp.mean(y, axis=-1, keepdims=True)
    yc = y - mu
    yn = yc * lax.rsqrt(jnp.mean(yc * yc, axis=-1, keepdims=True) + NORM_EPS)
    yn = yn * lng_ref[...] + lnb_ref[...]
    act = yn * jax.nn.sigmoid(yn)
    o_ref[...] = jnp.dot(act.astype(BF16), pw_ref[...], preferred_element_type=F32).astype(o_ref.dtype)


def conformer(proj, dw_w, dw_b, ln_g, ln_b, pw_all, layer, batch, seq, col0):
    m = proj.shape[0]
    width, ch = dw_w.shape
    half = ch // 2
    assert width - 1 <= CONV_HALO and col0 % half == 0
    ts = _tile(seq, 512)
    ns = seq // ts
    cb = col0 // half
    row = lambda b, s: b * ns + s
    col_spec = lambda j: pl.BlockSpec((ts, half), lambda b, s: (row(b, s), cb + j))
    full = lambda shape: pl.BlockSpec(shape, lambda b, s: (0, 0))
    return pl.pallas_call(
        functools.partial(_conformer_kernel, ts=ts, width=width, rows_per_chunk=64),
        grid=(batch, ns),
        in_specs=[col_spec(0), col_spec(1), col_spec(2), col_spec(3),
                  full((width, ch)), full((1, ch)), full((1, ch)), full((1, ch)),
                  pl.BlockSpec((None, ch, ch), lambda b, s: (layer, 0, 0))],
        out_specs=pl.BlockSpec((ts, ch), lambda b, s: (row(b, s), 0)),
        out_shape=jax.ShapeDtypeStruct((m, ch), BF16),
        scratch_shapes=[pltpu.VMEM((SUBLANES, ts + CONV_HALO, ch), F32), pltpu.VMEM((ts, ch), F32)],
        compiler_params=_params("parallel", "arbitrary"), name="conformer_conv",
    )(proj, proj, proj, proj, dw_w, dw_b.reshape(1, ch), ln_g.reshape(1, ch), ln_b.reshape(1, ch), pw_all)


def _retention_kernel(lg_ref, q_ref, k_ref, v_ref, g_ref, o_ref, state_ref, decay_ref, *, n_heads):
    c = RET_CHUNK
    scale = HEAD_DIM ** -0.5

    @pl.when(pl.program_id(1) == 0)
    def _():
        state_ref[...] = jnp.zeros(state_ref.shape, F32)
        row = lax.broadcasted_iota(jnp.int32, (c, c), 0)
        col = lax.broadcasted_iota(jnp.int32, (c, c), 1)
        rel = (row - col).astype(F32)
        for h in range(n_heads):
            decay_ref[h] = scale * jnp.where(row >= col, jnp.exp(lg_ref[h] * jnp.maximum(rel, 0.0)), 0.0)

    pos = lax.broadcasted_iota(jnp.int32, (c, 1), 0).astype(F32)
    chunk_len = jnp.full((1, 1), float(c), F32)
    for h in range(n_heads):
        hs = slice(h * HEAD_DIM, (h + 1) * HEAD_DIM)
        lg = lg_ref[h]
        q, k, v = q_ref[:, hs], k_ref[:, hs], v_ref[:, hs]
        s = lax.dot_general(q, k, (((1,), (1,)), ((), ())), preferred_element_type=F32)
        s = s * decay_ref[h]
        intra = jnp.dot(s.astype(BF16), v, preferred_element_type=F32)
        state = state_ref[h]
        q_decay = jnp.exp(lg * (pos + 1.0))
        cross = q_decay * jnp.dot(q, state.astype(BF16), preferred_element_type=F32)
        k_decay = jnp.exp(lg * (c - 1.0 - pos)) * scale
        kd = (k.astype(F32) * k_decay).astype(BF16)
        kv = lax.dot_general(kd, v, (((0,), (0,)), ((), ())), preferred_element_type=F32)
        state_ref[h] = state * jnp.exp(lg * chunk_len) + kv
        r = intra + cross
        mu = jnp.mean(r, axis=-1, keepdims=True)
        rc = r - mu
        rn = rc * lax.rsqrt(jnp.mean(rc * rc, axis=-1, keepdims=True) + NORM_EPS)
        gate = g_ref[:, hs].astype(F32)
        o_ref[:, hs] = (rn * (gate * jax.nn.sigmoid(gate))).astype(o_ref.dtype)


def retention(proj, log_gamma, batch, seq, n_heads, col0):
    m = proj.shape[0]
    w = n_heads * HEAD_DIM
    assert col0 % w == 0
    cb = col0 // w
    nc = seq // RET_CHUNK
    row = lambda b, n: b * nc + n
    spec = lambda j: pl.BlockSpec((RET_CHUNK, w), lambda b, n: (row(b, n), cb + j))
    return pl.pallas_call(
        functools.partial(_retention_kernel, n_heads=n_heads), grid=(batch, nc),
        in_specs=[pl.BlockSpec(memory_space=pltpu.SMEM), spec(0), spec(1), spec(2), spec(3)],
        out_specs=pl.BlockSpec((RET_CHUNK, w), lambda b, n: (row(b, n), 0)),
        out_shape=jax.ShapeDtypeStruct((m, w), BF16),
        scratch_shapes=[pltpu.VMEM((n_heads, HEAD_DIM, HEAD_DIM), F32),
                        pltpu.VMEM((n_heads, RET_CHUNK, RET_CHUNK), F32)],
        compiler_params=_params("parallel", "arbitrary"), name="retention",
    )(log_gamma, proj, proj, proj, proj)


def _mix_out_kernel(att_ref, conv_ref, ret_ref, w_ref, o_ref, wb_ref):
    _cast_weight_once(w_ref, wb_ref)
    ka, kc = att_ref.shape[1], conv_ref.shape[1]
    acc = jnp.dot(att_ref[...], wb_ref[0:ka, :], preferred_element_type=F32)
    acc += jnp.dot(conv_ref[...], wb_ref[ka:ka + kc, :], preferred_element_type=F32)
    acc += jnp.dot(ret_ref[...], wb_ref[ka + kc:, :], preferred_element_type=F32)
    o_ref[...] = acc.astype(o_ref.dtype)


def mix_out(att, conv, ret, w_all, layer):
    m = att.shape[0]
    _, k, n = w_all.shape
    tm, tn = _tile(m, 1024), _tile(n, 512)
    lhs = lambda a: pl.BlockSpec((tm, a.shape[1]), lambda j, i: (i, 0))
    return pl.pallas_call(
        _mix_out_kernel, grid=(n // tn, m // tm),
        in_specs=[lhs(att), lhs(conv), lhs(ret),
                  pl.BlockSpec((None, k, tn), lambda j, i: (layer, 0, j))],
        out_specs=pl.BlockSpec((tm, tn), lambda j, i: (i, j)),
        out_shape=jax.ShapeDtypeStruct((m, n), BF16),
        scratch_shapes=[pltpu.VMEM((k, tn), BF16)],
        compiler_params=_params("parallel", "arbitrary"), name="mix_out",
    )(att, conv, ret, w_all)


FFN_CARRY = 8


GELU_C1 = math.sqrt(2.0 / math.pi)
GELU_C2 = GELU_C1 * 0.044715


def _gelu_tanh(x):
    g = 0.5 * x
    return g + g * jnp.tanh(x * (GELU_C1 + GELU_C2 * (x * x)))


def _ffn_in_kernel(h_ref, wg_ref, wv_ref, dwg_ref, dwv_ref, bg_ref, bv_ref, o_ref,
                   wgb_ref, wvb_ref, ug_ref, uv_ref, *, tm, width, tiles_per_seq):
    i = pl.program_id(1)
    _cast_weight_once(wg_ref, wgb_ref)
    _cast_weight_once(wv_ref, wvb_ref)

    seq_start = (i % tiles_per_seq) == 0

    @pl.when(seq_start)
    def _():
        ug_ref[0:FFN_CARRY, :] = jnp.zeros((FFN_CARRY, ug_ref.shape[1]), F32)
        uv_ref[0:FFN_CARRY, :] = jnp.zeros((FFN_CARRY, uv_ref.shape[1]), F32)

    @pl.when(jnp.logical_not(seq_start))
    def _():
        ug_ref[0:FFN_CARRY, :] = ug_ref[tm:tm + FFN_CARRY, :]
        uv_ref[0:FFN_CARRY, :] = uv_ref[tm:tm + FFN_CARRY, :]

    h = h_ref[...]
    ug_ref[FFN_CARRY:, :] = jnp.dot(h, wgb_ref[...], preferred_element_type=F32)
    uv_ref[FFN_CARRY:, :] = jnp.dot(h, wvb_ref[...], preferred_element_type=F32)

    def conv(u_ref, dw_ref, b_ref):
        u = u_ref[...]
        acc = dw_ref[0:1, :] * u
        for k in range(1, width):
            acc = dw_ref[k:k + 1, :] * u + pltpu.roll(acc, 1, axis=0)
        return acc[FFN_CARRY:, :] + b_ref[...]

    gate = conv(ug_ref, dwg_ref, bg_ref)
    val = conv(uv_ref, dwv_ref, bv_ref)
    o_ref[...] = (_gelu_tanh(gate) * val).astype(o_ref.dtype)


def ffn_in(h, w_all, layer, dw_w, dw_b, seq):
    m, k = h.shape
    width, n2 = dw_w.shape
    d_ff = n2 // 2
    assert width - 1 <= FFN_CARRY
    tm, tn = _tile(seq, 1024), _tile(d_ff, 256)
    nj = d_ff // tn
    u_scratch = pltpu.VMEM((tm + FFN_CARRY, tn), F32)
    return pl.pallas_call(
        functools.partial(_ffn_in_kernel, tm=tm, width=width, tiles_per_seq=seq // tm),
        grid=(nj, m // tm),
        in_specs=[pl.BlockSpec((tm, k), lambda j, i: (i, 0)),
                  pl.BlockSpec((None, k, tn), lambda j, i: (layer, 0, j)),
                  pl.BlockSpec((None, k, tn), lambda j, i: (layer, 0, j + nj)),
                  pl.BlockSpec((width, tn), lambda j, i: (0, j)),
                  pl.BlockSpec((width, tn), lambda j, i: (0, j + nj)),
                  pl.BlockSpec((1, tn), lambda j, i: (0, j)),
                  pl.BlockSpec((1, tn), lambda j, i: (0, j + nj))],
        out_specs=pl.BlockSpec((tm, tn), lambda j, i: (i, j)),
        out_shape=jax.ShapeDtypeStruct((m, d_ff), BF16),
        scratch_shapes=[pltpu.VMEM((k, tn), BF16), pltpu.VMEM((k, tn), BF16), u_scratch, u_scratch],
        compiler_params=_params("parallel", "arbitrary"), name="ffn_in",
    )(h, w_all, w_all, dw_w, dw_w, dw_b.reshape(1, n2), dw_b.reshape(1, n2))


def ffn_out(a, w_all, layer):
    m, k = a.shape
    n = w_all.shape[2]
    tm, tn = _tile(m, 512), _tile(n, 512)
    return pl.pallas_call(
        _matmul_kernel, grid=(m // tm, n // tn),
        in_specs=[pl.BlockSpec((tm, k), lambda i, j: (i, 0)),
                  pl.BlockSpec((None, k, tn), lambda i, j: (layer, 0, j))],
        out_specs=pl.BlockSpec((tm, tn), lambda i, j: (i, j)),
        out_shape=jax.ShapeDtypeStruct((m, n), BF16),
        compiler_params=_params("parallel", "arbitrary"), name="ffn_out",
    )(a, w_all)


CAST_BLOCK_BYTES = 8 * 1024 * 1024


def _cast_kernel(w_ref, o_ref):
    o_ref[...] = w_ref[...].astype(o_ref.dtype)


def cast_bf16(w_all):
    depth, k, n = w_all.shape
    tk = _tile(k, max(16, CAST_BLOCK_BYTES // (4 * n)))
    spec = pl.BlockSpec((None, tk, n), lambda l, i: (l, i, 0))
    return pl.pallas_call(
        _cast_kernel, grid=(depth, k // tk), in_specs=[spec], out_specs=spec,
        out_shape=jax.ShapeDtypeStruct(w_all.shape, BF16),
        compiler_params=_params("parallel", "parallel"), name="cast_bf16",
    )(w_all)


def _pow2_slopes(n):
    start = 2.0 ** (-8.0 / n)
    return [start ** (i + 1) for i in range(n)]


def _alibi_slopes(n):
    if math.log2(n).is_integer():
        return _pow2_slopes(n)
    c = 2 ** math.floor(math.log2(n))
    return _pow2_slopes(c) + _alibi_slopes(2 * c)[0::2][: n - c]


def kernel(x, mix_pre_g, mix_post_g, ffn_pre_g, ffn_post_g, w_in, att_sinks, conv_dw_w, conv_dw_b,
           conv_ln_g, conv_ln_b, conv_pw_w, w_out, ffn_w_in, ffn_dw_w, ffn_dw_b, ffn_w_out):
    b, s, d = x.shape
    depth = w_in.shape[0]
    att_heads = att_sinks.shape[1]
    conv_ch = conv_dw_b.shape[1]
    att_q_cols = att_heads * HEAD_DIM
    att_kv_cols = att_q_cols // GQA_RATIO
    ret_heads = (w_in.shape[2] - att_q_cols - 2 * att_kv_cols - 2 * conv_ch) // (4 * HEAD_DIM)
    conv_col0 = att_q_cols + 2 * att_kv_cols
    ret_col0 = conv_col0 + 2 * conv_ch
    assert s % WINDOW == 0 and s % RET_CHUNK == 0

    slopes = jnp.asarray(_alibi_slopes(att_heads), dtype=F32)
    log_gamma = jnp.log1p(-jnp.exp2(-5.0 - jnp.arange(ret_heads, dtype=F32)))

    conv_pw_bf16 = cast_bf16(conv_pw_w)
    ffn_w_out_bf16 = cast_bf16(ffn_w_out)

    xf = x.reshape(b * s, d)
    y = None
    for l in range(depth):
        if l == 0:
            h = prenorm(xf, mix_pre_g[l])
        else:
            xf, h = resid_prenorm(xf, y, ffn_post_g[l - 1], mix_pre_g[l])
        proj = in_proj(h, w_in, l)
        att = attention(proj, att_sinks[l], slopes, b, s, att_heads)
        conv = conformer(proj, conv_dw_w[l], conv_dw_b[l], conv_ln_g[l], conv_ln_b[l],
                         conv_pw_bf16, l, b, s, conv_col0)
        ret = retention(proj, log_gamma, b, s, ret_heads, ret_col0)
        y = mix_out(att, conv, ret, w_out, l)
        xf, h = resid_prenorm(xf, y, mix_post_g[l], ffn_pre_g[l])
        a = ffn_in(h, ffn_w_in, l, ffn_dw_w[l], ffn_dw_b[l], s)
        y = ffn_out(a, ffn_w_out_bf16, l)
    xf = resid(xf, y, ffn_post_g[depth - 1])
    return xf.reshape(b, s, d)
```

```python
import functools
import math

import jax
import jax.numpy as jnp
from jax import lax
from jax.experimental import pallas as pl
from jax.experimental.pallas import tpu as pltpu

HEAD_DIM = 128
WINDOW = 128
RET_CHUNK = 128
NORM_EPS = 1e-6
NEG_INF = -1e30
GQA_RATIO = 3

F32 = jnp.float32
BF16 = jnp.bfloat16

V7X_VMEM_BYTES = 64 * 1024 * 1024
VMEM_LIMIT_BYTES = 60 * 1024 * 1024


def _params(*semantics, flags=None):
    return pltpu.CompilerParams(dimension_semantics=semantics,
                                vmem_limit_bytes=VMEM_LIMIT_BYTES, flags=flags)


def _tile(n, pref):
    t = min(n, pref)
    while n % t:
        t //= 2
    return t


def _rms(v):
    return v * lax.rsqrt(jnp.mean(v * v, axis=-1, keepdims=True) + NORM_EPS)


def _prenorm_kernel(x_ref, g_ref, h_ref):
    h_ref[...] = (_rms(x_ref[...]) * g_ref[...]).astype(h_ref.dtype)


def _resid_prenorm_kernel(x_ref, y_ref, gpost_ref, gpre_ref, xo_ref, h_ref):
    x = x_ref[...] + _rms(y_ref[...].astype(F32)) * gpost_ref[...]
    xo_ref[...] = x
    h_ref[...] = (_rms(x) * gpre_ref[...]).astype(h_ref.dtype)


def _resid_kernel(x_ref, y_ref, gpost_ref, xo_ref):
    xo_ref[...] = x_ref[...] + _rms(y_ref[...].astype(F32)) * gpost_ref[...]


def _row_spec(tr, d):
    return pl.BlockSpec((tr, d), lambda i: (i, 0))


def _gain_spec(d):
    return pl.BlockSpec((1, d), lambda i: (0, 0))


def prenorm(x, g):
    m, d = x.shape
    tr = _tile(m, 256)
    return pl.pallas_call(
        _prenorm_kernel, grid=(m // tr,),
        in_specs=[_row_spec(tr, d), _gain_spec(d)],
        out_specs=_row_spec(tr, d),
        out_shape=jax.ShapeDtypeStruct((m, d), BF16),
        compiler_params=_params("parallel"), name="prenorm",
    )(x, g.reshape(1, d))


def resid_prenorm(x, y, gpost, gpre):
    m, d = x.shape
    tr = _tile(m, 256)
    return pl.pallas_call(
        _resid_prenorm_kernel, grid=(m // tr,),
        in_specs=[_row_spec(tr, d), _row_spec(tr, d), _gain_spec(d), _gain_spec(d)],
        out_specs=[_row_spec(tr, d), _row_spec(tr, d)],
        out_shape=[jax.ShapeDtypeStruct((m, d), F32), jax.ShapeDtypeStruct((m, d), BF16)],
        compiler_params=_params("parallel"), name="resid_prenorm",
    )(x, y, gpost.reshape(1, d), gpre.reshape(1, d))


def resid(x, y, gpost):
    m, d = x.shape
    tr = _tile(m, 256)
    return pl.pallas_call(
        _resid_kernel, grid=(m // tr,),
        in_specs=[_row_spec(tr, d), _row_spec(tr, d), _gain_spec(d)],
        out_specs=_row_spec(tr, d),
        out_shape=jax.ShapeDtypeStruct((m, d), F32),
        compiler_params=_params("parallel"), name="resid",
    )(x, y, gpost.reshape(1, d))


def _matmul_kernel(a_ref, b_ref, o_ref):
    o_ref[...] = jnp.dot(a_ref[...], b_ref[...], preferred_element_type=F32).astype(o_ref.dtype)


def _cast_weight_once(w_ref, wb_ref):
    @pl.when(pl.program_id(1) == 0)
    def _():
        wb_ref[...] = w_ref[...].astype(BF16)


def _in_proj_kernel(a_ref, w_ref, o_ref, wb_ref):
    _cast_weight_once(w_ref, wb_ref)
    o_ref[...] = jnp.dot(a_ref[...], wb_ref[...], preferred_element_type=F32).astype(o_ref.dtype)


def in_proj(h, w_all, layer):
    m, k = h.shape
    n = w_all.shape[2]
    tm, tn = _tile(m, 1024), _tile(n, 768)
    return pl.pallas_call(
        _in_proj_kernel, grid=(n // tn, m // tm),
        in_specs=[pl.BlockSpec((tm, k), lambda j, i: (i, 0)),
                  pl.BlockSpec((None, k, tn), lambda j, i: (layer, 0, j))],
        out_specs=pl.BlockSpec((tm, tn), lambda j, i: (i, j)),
        out_shape=jax.ShapeDtypeStruct((m, n), BF16),
        scratch_shapes=[pltpu.VMEM((k, tn), BF16)],
        compiler_params=_params("parallel", "arbitrary"), name="in_proj",
    )(h, w_all)


LOG2E = math.log2(math.e)


def _attn_kernel(sink_ref, slope_ref, q_ref, kc_ref, vc_ref, kp_ref, vp_ref, o_ref, bias_ref, *, n_kv):
    n = pl.program_id(1)
    blk = WINDOW
    n_heads = n_kv * GQA_RATIO

    @pl.when(n == 0)
    def _():
        qi = lax.broadcasted_iota(jnp.int32, (blk, 2 * blk), 0)
        kj = lax.broadcasted_iota(jnp.int32, (blk, 2 * blk), 1)
        dist = qi + blk - kj
        in_window = (dist >= 0) & (dist < WINDOW)
        distf = dist.astype(F32)
        for first, valid in ((0, in_window & (kj >= blk)), (1, in_window)):
            for h in range(n_heads):
                bias_ref[first, h] = jnp.where(valid, (-LOG2E * slope_ref[h]) * distf, NEG_INF)

    later = jnp.minimum(n, 1)
    scale = HEAD_DIM ** -0.5 * LOG2E
    for hk in range(n_kv):
        ks = slice(hk * HEAD_DIM, (hk + 1) * HEAD_DIM)
        k = jnp.concatenate([kp_ref[:, ks], kc_ref[:, ks]], axis=0)
        v = jnp.concatenate([vp_ref[:, ks], vc_ref[:, ks]], axis=0)
        for g in range(GQA_RATIO):
            h = hk * GQA_RATIO + g
            hs = slice(h * HEAD_DIM, (h + 1) * HEAD_DIM)
            s = lax.dot_general(q_ref[:, hs], k, (((1,), (1,)), ((), ())),
                                preferred_element_type=F32)
            s = s * scale + bias_ref[later, h]
            sink = sink_ref[h] * LOG2E
            mx = jnp.maximum(jnp.max(s, axis=-1, keepdims=True), sink)
            p = jnp.exp2(s - mx)
            denom = jnp.sum(p, axis=-1, keepdims=True) + jnp.exp2(sink - mx)
            out = jnp.dot(p.astype(BF16), v, preferred_element_type=F32) / denom
            o_ref[:, hs] = out.astype(o_ref.dtype)


def attention(proj, sinks, slopes, batch, seq, n_heads):
    m = proj.shape[0]
    n_kv = n_heads // GQA_RATIO
    qw, kw = n_heads * HEAD_DIM, n_kv * HEAD_DIM
    nb = seq // WINDOW
    k_blk, v_blk = qw // kw, qw // kw + 1
    cur = lambda b, n: b * nb + n
    prev = lambda b, n: b * nb + jnp.maximum(n - 1, 0)
    smem = pl.BlockSpec(memory_space=pltpu.SMEM)
    return pl.pallas_call(
        functools.partial(_attn_kernel, n_kv=n_kv), grid=(batch, nb),
        in_specs=[smem, smem,
                  pl.BlockSpec((WINDOW, qw), lambda b, n: (cur(b, n), 0)),
                  pl.BlockSpec((WINDOW, kw), lambda b, n: (cur(b, n), k_blk)),
                  pl.BlockSpec((WINDOW, kw), lambda b, n: (cur(b, n), v_blk)),
                  pl.BlockSpec((WINDOW, kw), lambda b, n: (prev(b, n), k_blk)),
                  pl.BlockSpec((WINDOW, kw), lambda b, n: (prev(b, n), v_blk))],
        out_specs=pl.BlockSpec((WINDOW, qw), lambda b, n: (cur(b, n), 0)),
        out_shape=jax.ShapeDtypeStruct((m, qw), BF16),
        scratch_shapes=[pltpu.VMEM((2, n_heads, WINDOW, 2 * WINDOW), F32)],
        compiler_params=_params("parallel", "arbitrary"), name="swa_attention",
    )(sinks, slopes, proj, proj, proj, proj, proj)


CONV_HALO = 32
SUBLANES = 8


def _conformer_kernel(a0_ref, a1_ref, g0_ref, g1_ref, dww_ref, dwb_ref, lng_ref, lnb_ref,
                      pw_ref, o_ref, ysh_ref, cbuf_ref, *, ts, width, rows_per_chunk):
    half = a0_ref.shape[1]
    ch = 2 * half
    buf_rows = ts + CONV_HALO

    @pl.when(pl.program_id(1) == 0)
    def _():
        ysh_ref[0, 0:CONV_HALO, :] = jnp.zeros((CONV_HALO, ch), F32)

    for c, (a_ref, g_ref) in enumerate(((a0_ref, g0_ref), (a1_ref, g1_ref))):
        a = a_ref[...].astype(F32)
        g = g_ref[...].astype(F32)
        ysh_ref[0, CONV_HALO:, c * half:(c + 1) * half] = a * jax.nn.sigmoid(g)

    y0 = ysh_ref[0]
    for b in range(1, SUBLANES):
        ysh_ref[b, SUBLANES:, :] = pltpu.roll(y0, b, axis=0)[SUBLANES:, :]

    lanes = 128

    n_aligned = -(-width // SUBLANES)
    back = SUBLANES * (n_aligned - 1)

    def conv_chunk(r, carry):
        r0 = pl.multiple_of(r * rows_per_chunk, rows_per_chunk)
        for c0 in range(0, ch, lanes):
            acc = jnp.broadcast_to(dwb_ref[:, c0:c0 + lanes], (rows_per_chunk, lanes))
            for b in range(SUBLANES):
                ext = ysh_ref[b, pl.ds(r0 + CONV_HALO - back, rows_per_chunk + back), c0:c0 + lanes]
                for a in range(n_aligned):
                    delay = SUBLANES * a + b
                    if delay < width:
                        k = width - 1 - delay
                        lo = back - SUBLANES * a
                        acc = acc + dww_ref[k:k + 1, c0:c0 + lanes] * ext[lo:lo + rows_per_chunk]
            cbuf_ref[pl.ds(r0, rows_per_chunk), c0:c0 + lanes] = acc
        return carry

    lax.fori_loop(0, ts // rows_per_chunk, conv_chunk, 0)
    ysh_ref[0, 0:CONV_HALO, :] = ysh_ref[0, ts:buf_rows, :]

    y = cbuf_ref[...]
    mu = jnp.mean(y, axis=-1, keepdims=True)
    yc = y - mu
    yn = yc * lax.rsqrt(jnp.mean(yc * yc, axis=-1, keepdims=True) + NORM_EPS)
    yn = yn * lng_ref[...] + lnb_ref[...]
    act = yn * jax.nn.sigmoid(yn)
    o_ref[...] = jnp.dot(act.astype(BF16), pw_ref[...], preferred_element_type=F32).astype(o_ref.dtype)


def conformer(proj, dw_w, dw_b, ln_g, ln_b, pw_all, layer, batch, seq, col0):
    m = proj.shape[0]
    width, ch = dw_w.shape
    half = ch // 2
    assert width - 1 <= CONV_HALO and col0 % half == 0
    ts = _tile(seq, 512)
    ns = seq // ts
    cb = col0 // half
    row = lambda b, s: b * ns + s
    col_spec = lambda j: pl.BlockSpec((ts, half), lambda b, s: (row(b, s), cb + j))
    full = lambda shape: pl.BlockSpec(shape, lambda b, s: (0, 0))
    return pl.pallas_call(
        functools.partial(_conformer_kernel, ts=ts, width=width, rows_per_chunk=64),
        grid=(batch, ns),
        in_specs=[col_spec(0), col_spec(1), col_spec(2), col_spec(3),
                  full((width, ch)), full((1, ch)), full((1, ch)), full((1, ch)),
                  pl.BlockSpec((None, ch, ch), lambda b, s: (layer, 0, 0))],
        out_specs=pl.BlockSpec((ts, ch), lambda b, s: (row(b, s), 0)),
        out_shape=jax.ShapeDtypeStruct((m, ch), BF16),
        scratch_shapes=[pltpu.VMEM((SUBLANES, ts + CONV_HALO, ch), F32), pltpu.VMEM((ts, ch), F32)],
        compiler_params=_params("parallel", "arbitrary"), name="conformer_conv",
    )(proj, proj, proj, proj, dw_w, dw_b.reshape(1, ch), ln_g.reshape(1, ch), ln_b.reshape(1, ch), pw_all)


def _retention_kernel(lg_ref, q_ref, k_ref, v_ref, g_ref, o_ref, state_ref, decay_ref, *, n_heads):
    c = RET_CHUNK
    scale = HEAD_DIM ** -0.5

    @pl.when(pl.program_id(1) == 0)
    def _():
        state_ref[...] = jnp.zeros(state_ref.shape, F32)
        row = lax.broadcasted_iota(jnp.int32, (c, c), 0)
        col = lax.broadcasted_iota(jnp.int32, (c, c), 1)
        rel = (row - col).astype(F32)
        for h in range(n_heads):
            decay_ref[h] = scale * jnp.where(row >= col, jnp.exp(lg_ref[h] * jnp.maximum(rel, 0.0)), 0.0)

    pos = lax.broadcasted_iota(jnp.int32, (c, 1), 0).astype(F32)
    chunk_len = jnp.full((1, 1), float(c), F32)
    for h in range(n_heads):
        hs = slice(h * HEAD_DIM, (h + 1) * HEAD_DIM)
        lg = lg_ref[h]
        q, k, v = q_ref[:, hs], k_ref[:, hs], v_ref[:, hs]
        s = lax.dot_general(q, k, (((1,), (1,)), ((), ())), preferred_element_type=F32)
        s = s * decay_ref[h]
        intra = jnp.dot(s.astype(BF16), v, preferred_element_type=F32)
        state = state_ref[h]
        q_decay = jnp.exp(lg * (pos + 1.0))
        cross = q_decay * jnp.dot(q, state.astype(BF16), preferred_element_type=F32)
        k_decay = jnp.exp(lg * (c - 1.0 - pos)) * scale
        kd = (k.astype(F32) * k_decay).astype(BF16)
        kv = lax.dot_general(kd, v, (((0,), (0,)), ((), ())), preferred_element_type=F32)
        state_ref[h] = state * jnp.exp(lg * chunk_len) + kv
        r = intra + cross
        mu = jnp.mean(r, axis=-1, keepdims=True)
        rc = r - mu
        rn = rc * lax.rsqrt(jnp.mean(rc * rc, axis=-1, keepdims=True) + NORM_EPS)
        gate = g_ref[:, hs].astype(F32)
        o_ref[:, hs] = (rn * (gate * jax.nn.sigmoid(gate))).astype(o_ref.dtype)


def retention(proj, log_gamma, batch, seq, n_heads, col0):
    m = proj.shape[0]
    w = n_heads * HEAD_DIM
    assert col0 % w == 0
    cb = col0 // w
    nc = seq // RET_CHUNK
    row = lambda b, n: b * nc + n
    spec = lambda j: pl.BlockSpec((RET_CHUNK, w), lambda b, n: (row(b, n), cb + j))
    return pl.pallas_call(
        functools.partial(_retention_kernel, n_heads=n_heads), grid=(batch, nc),
        in_specs=[pl.BlockSpec(memory_space=pltpu.SMEM), spec(0), spec(1), spec(2), spec(3)],
        out_specs=pl.BlockSpec((RET_CHUNK, w), lambda b, n: (row(b, n), 0)),
        out_shape=jax.ShapeDtypeStruct((m, w), BF16),
        scratch_shapes=[pltpu.VMEM((n_heads, HEAD_DIM, HEAD_DIM), F32),
                        pltpu.VMEM((n_heads, RET_CHUNK, RET_CHUNK), F32)],
        compiler_params=_params("parallel", "arbitrary"), name="retention",
    )(log_gamma, proj, proj, proj, proj)


def _mix_out_kernel(att_ref, conv_ref, ret_ref, w_ref, o_ref, wb_ref):
    _cast_weight_once(w_ref, wb_ref)
    ka, kc = att_ref.shape[1], conv_ref.shape[1]
    acc = jnp.dot(att_ref[...], wb_ref[0:ka, :], preferred_element_type=F32)
    acc += jnp.dot(conv_ref[...], wb_ref[ka:ka + kc, :], preferred_element_type=F32)
    acc += jnp.dot(ret_ref[...], wb_ref[ka + kc:, :], preferred_element_type=F32)
    o_ref[...] = acc.astype(o_ref.dtype)


def mix_out(att, conv, ret, w_all, layer):
    m = att.shape[0]
    _, k, n = w_all.shape
    tm, tn = _tile(m, 1024), _tile(n, 512)
    lhs = lambda a: pl.BlockSpec((tm, a.shape[1]), lambda j, i: (i, 0))
    return pl.pallas_call(
        _mix_out_kernel, grid=(n // tn, m // tm),
        in_specs=[lhs(att), lhs(conv), lhs(ret),
                  pl.BlockSpec((None, k, tn), lambda j, i: (layer, 0, j))],
        out_specs=pl.BlockSpec((tm, tn), lambda j, i: (i, j)),
        out_shape=jax.ShapeDtypeStruct((m, n), BF16),
        scratch_shapes=[pltpu.VMEM((k, tn), BF16)],
        compiler_params=_params("parallel", "arbitrary"), name="mix_out",
    )(att, conv, ret, w_all)


FFN_CARRY = 8


GELU_C1 = math.sqrt(2.0 / math.pi)
GELU_C2 = GELU_C1 * 0.044715


def _gelu_tanh(x):
    g = 0.5 * x
    return g + g * jnp.tanh(x * (GELU_C1 + GELU_C2 * (x * x)))


def _ffn_in_kernel(h_ref, wg_ref, wv_ref, dwg_ref, dwv_ref, bg_ref, bv_ref, o_ref,
                   wgb_ref, wvb_ref, ug_ref, uv_ref, *, tm, width, tiles_per_seq):
    i = pl.program_id(1)
    _cast_weight_once(wg_ref, wgb_ref)
    _cast_weight_once(wv_ref, wvb_ref)

    seq_start = (i % tiles_per_seq) == 0

    @pl.when(seq_start)
    def _():
        ug_ref[0:FFN_CARRY, :] = jnp.zeros((FFN_CARRY, ug_ref.shape[1]), F32)
        uv_ref[0:FFN_CARRY, :] = jnp.zeros((FFN_CARRY, uv_ref.shape[1]), F32)

    @pl.when(jnp.logical_not(seq_start))
    def _():
        ug_ref[0:FFN_CARRY, :] = ug_ref[tm:tm + FFN_CARRY, :]
        uv_ref[0:FFN_CARRY, :] = uv_ref[tm:tm + FFN_CARRY, :]

    h = h_ref[...]
    ug_ref[FFN_CARRY:, :] = jnp.dot(h, wgb_ref[...], preferred_element_type=F32)
    uv_ref[FFN_CARRY:, :] = jnp.dot(h, wvb_ref[...], preferred_element_type=F32)

    def conv(u_ref, dw_ref, b_ref):
        u = u_ref[...]
        acc = dw_ref[0:1, :] * u
        for k in range(1, width):
            acc = dw_ref[k:k + 1, :] * u + pltpu.roll(acc, 1, axis=0)
        return acc[FFN_CARRY:, :] + b_ref[...]

    gate = conv(ug_ref, dwg_ref, bg_ref)
    val = conv(uv_ref, dwv_ref, bv_ref)
    o_ref[...] = (_gelu_tanh(gate) * val).astype(o_ref.dtype)


def ffn_in(h, w_all, layer, dw_w, dw_b, seq):
    m, k = h.shape
    width, n2 = dw_w.shape
    d_ff = n2 // 2
    assert width - 1 <= FFN_CARRY
    tm, tn = _tile(seq, 1024), _tile(d_ff, 256)
    nj = d_ff // tn
    u_scratch = pltpu.VMEM((tm + FFN_CARRY, tn), F32)
    return pl.pallas_call(
        functools.partial(_ffn_in_kernel, tm=tm, width=width, tiles_per_seq=seq // tm),
        grid=(nj, m // tm),
        in_specs=[pl.BlockSpec((tm, k), lambda j, i: (i, 0)),
                  pl.BlockSpec((None, k, tn), lambda j, i: (layer, 0, j)),
                  pl.BlockSpec((None, k, tn), lambda j, i: (layer, 0, j + nj)),
                  pl.BlockSpec((width, tn), lambda j, i: (0, j)),
                  pl.BlockSpec((width, tn), lambda j, i: (0, j + nj)),
                  pl.BlockSpec((1, tn), lambda j, i: (0, j)),
                  pl.BlockSpec((1, tn), lambda j, i: (0, j + nj))],
        out_specs=pl.BlockSpec((tm, tn), lambda j, i: (i, j)),
        out_shape=jax.ShapeDtypeStruct((m, d_ff), BF16),
        scratch_shapes=[pltpu.VMEM((k, tn), BF16), pltpu.VMEM((k, tn), BF16), u_scratch, u_scratch],
        compiler_params=_params("parallel", "arbitrary"), name="ffn_in",
    )(h, w_all, w_all, dw_w, dw_w, dw_b.reshape(1, n2), dw_b.reshape(1, n2))


def ffn_out(a, w_all, layer):
    m, k = a.shape
    n = w_all.shape[2]
    tm, tn = _tile(m, 512), _tile(n, 512)
    return pl.pallas_call(
        _matmul_kernel, grid=(m // tm, n // tn),
        in_specs=[pl.BlockSpec((tm, k), lambda i, j: (i, 0)),
                  pl.BlockSpec((None, k, tn), lambda i, j: (layer, 0, j))],
        out_specs=pl.BlockSpec((tm, tn), lambda i, j: (i, j)),
        out_shape=jax.ShapeDtypeStruct((m, n), BF16),
        compiler_params=_params("parallel", "arbitrary"), name="ffn_out",
    )(a, w_all)


CAST_BLOCK_BYTES = 8 * 1024 * 1024


def _cast_kernel(w_ref, o_ref):
    o_ref[...] = w_ref[...].astype(o_ref.dtype)


def cast_bf16(w_all):
    depth, k, n = w_all.shape
    tk = _tile(k, max(16, CAST_BLOCK_BYTES // (4 * n)))
    spec = pl.BlockSpec((None, tk, n), lambda l, i: (l, i, 0))
    return pl.pallas_call(
        _cast_kernel, grid=(depth, k // tk), in_specs=[spec], out_specs=spec,
        out_shape=jax.ShapeDtypeStruct(w_all.shape, BF16),
        compiler_params=_params("parallel", "parallel"), name="cast_bf16",
    )(w_all)


def _pow2_slopes(n):
    start = 2.0 ** (-8.0 / n)
    return [start ** (i + 1) for i in range(n)]


def _alibi_slopes(n):
    if math.log2(n).is_integer():
        return _pow2_slopes(n)
    c = 2 ** math.floor(math.log2(n))
    return _pow2_slopes(c) + _alibi_slopes(2 * c)[0::2][: n - c]


def kernel(x, mix_pre_g, mix_post_g, ffn_pre_g, ffn_post_g, w_in, att_sinks, conv_dw_w, conv_dw_b,
           conv_ln_g, conv_ln_b, conv_pw_w, w_out, ffn_w_in, ffn_dw_w, ffn_dw_b, ffn_w_out):
    b, s, d = x.shape
    depth = w_in.shape[0]
    att_heads = att_sinks.shape[1]
    conv_ch = conv_dw_b.shape[1]
    att_q_cols = att_heads * HEAD_DIM
    att_kv_cols = att_q_cols // GQA_RATIO
    ret_heads = (w_in.shape[2] - att_q_cols - 2 * att_kv_cols - 2 * conv_ch) // (4 * HEAD_DIM)
    conv_col0 = att_q_cols + 2 * att_kv_cols
    ret_col0 = conv_col0 + 2 * conv_ch
    assert s % WINDOW == 0 and s % RET_CHUNK == 0

    slopes = jnp.asarray(_alibi_slopes(att_heads), dtype=F32)
    log_gamma = jnp.log1p(-jnp.exp2(-5.0 - jnp.arange(ret_heads, dtype=F32)))

    conv_pw_bf16 = cast_bf16(conv_pw_w)
    ffn_w_out_bf16 = cast_bf16(ffn_w_out)

    xf = x.reshape(b * s, d)
    y = None
    for l in range(depth):
        if l == 0:
            h = prenorm(xf, mix_pre_g[l])
        else:
            xf, h = resid_prenorm(xf, y, ffn_post_g[l - 1], mix_pre_g[l])
        proj = in_proj(h, w_in, l)
        att = attention(proj, att_sinks[l], slopes, b, s, att_heads)
        conv = conformer(proj, conv_dw_w[l], conv_dw_b[l], conv_ln_g[l], conv_ln_b[l],
                         conv_pw_bf16, l, b, s, conv_col0)
        ret = retention(proj, log_gamma, b, s, ret_heads, ret_col0)
        y = mix_out(att, conv, ret, w_out, l)
        xf, h = resid_prenorm(xf, y, mix_post_g[l], ffn_pre_g[l])
        a = ffn_in(h, ffn_w_in, l, ffn_dw_w[l], ffn_dw_b[l], s)
        y = ffn_out(a, ffn_w_out_bf16, l)
    xf = resid(xf, y, ffn_post_g[depth - 1])
    return xf.reshape(b, s, d)
```

```python
import functools
import math

import jax
import jax.numpy as jnp
from jax import lax
from jax.experimental import pallas as pl
from jax.experimental.pallas import tpu as pltpu

HEAD_DIM = 128
WINDOW = 128
RET_CHUNK = 128
NORM_EPS = 1e-6
NEG_INF = -1e30
GQA_RATIO = 3

F32 = jnp.float32
BF16 = jnp.bfloat16

V7X_VMEM_BYTES = 64 * 1024 * 1024
VMEM_LIMIT_BYTES = 60 * 1024 * 1024


def _params(*semantics, flags=None):
    return pltpu.CompilerParams(dimension_semantics=semantics,
                                vmem_limit_bytes=VMEM_LIMIT_BYTES, flags=flags)


def _tile(n, pref):
    t = min(n, pref)
    while n % t:
        t //= 2
    return t


def _rms(v):
    return v * lax.rsqrt(jnp.mean(v * v, axis=-1, keepdims=True) + NORM_EPS)


def _prenorm_kernel(x_ref, g_ref, h_ref):
    h_ref[...] = (_rms(x_ref[...]) * g_ref[...]).astype(h_ref.dtype)


def _resid_prenorm_kernel(x_ref, y_ref, gpost_ref, gpre_ref, xo_ref, h_ref):
    x = x_ref[...] + _rms(y_ref[...].astype(F32)) * gpost_ref[...]
    xo_ref[...] = x
    h_ref[...] = (_rms(x) * gpre_ref[...]).astype(h_ref.dtype)


def _resid_kernel(x_ref, y_ref, gpost_ref, xo_ref):
    xo_ref[...] = x_ref[...] + _rms(y_ref[...].astype(F32)) * gpost_ref[...]


def _row_spec(tr, d):
    return pl.BlockSpec((tr, d), lambda i: (i, 0))


def _gain_spec(d):
    return pl.BlockSpec((1, d), lambda i: (0, 0))


def prenorm(x, g):
    m, d = x.shape
    tr = _tile(m, 256)
    return pl.pallas_call(
        _prenorm_kernel, grid=(m // tr,),
        in_specs=[_row_spec(tr, d), _gain_spec(d)],
        out_specs=_row_spec(tr, d),
        out_shape=jax.ShapeDtypeStruct((m, d), BF16),
        compiler_params=_params("parallel"), name="prenorm",
    )(x, g.reshape(1, d))


def resid_prenorm(x, y, gpost, gpre):
    m, d = x.shape
    tr = _tile(m, 256)
    return pl.pallas_call(
        _resid_prenorm_kernel, grid=(m // tr,),
        in_specs=[_row_spec(tr, d), _row_spec(tr, d), _gain_spec(d), _gain_spec(d)],
        out_specs=[_row_spec(tr, d), _row_spec(tr, d)],
        out_shape=[jax.ShapeDtypeStruct((m, d), F32), jax.ShapeDtypeStruct((m, d), BF16)],
        compiler_params=_params("parallel"), name="resid_prenorm",
    )(x, y, gpost.reshape(1, d), gpre.reshape(1, d))


def resid(x, y, gpost):
    m, d = x.shape
    tr = _tile(m, 256)
    return pl.pallas_call(
        _resid_kernel, grid=(m // tr,),
        in_specs=[_row_spec(tr, d), _row_spec(tr, d), _gain_spec(d)],
        out_specs=_row_spec(tr, d),
        out_shape=jax.ShapeDtypeStruct((m, d), F32),
        compiler_params=_params("parallel"), name="resid",
    )(x, y, gpost.reshape(1, d))


def _matmul_kernel(a_ref, b_ref, o_ref):
    o_ref[...] = jnp.dot(a_ref[...], b_ref[...], preferred_element_type=F32).astype(o_ref.dtype)


def _cast_weight_once(w_ref, wb_ref):
    @pl.when(pl.program_id(1) == 0)
    def _():
        wb_ref[...] = w_ref[...].astype(BF16)


def _in_proj_kernel(a_ref, w_ref, o_ref, wb_ref):
    _cast_weight_once(w_ref, wb_ref)
    o_ref[...] = jnp.dot(a_ref[...], wb_ref[...], preferred_element_type=F32).astype(o_ref.dtype)


def in_proj(h, w_all, layer):
    m, k = h.shape
    n = w_all.shape[2]
    tm, tn = _tile(m, 1024), _tile(n, 768)
    return pl.pallas_call(
        _in_proj_kernel, grid=(n // tn, m // tm),
        in_specs=[pl.BlockSpec((tm, k), lambda j, i: (i, 0)),
                  pl.BlockSpec((None, k, tn), lambda j, i: (layer, 0, j))],
        out_specs=pl.BlockSpec((tm, tn), lambda j, i: (i, j)),
        out_shape=jax.ShapeDtypeStruct((m, n), BF16),
        scratch_shapes=[pltpu.VMEM((k, tn), BF16)],
        compiler_params=_params("parallel", "arbitrary"), name="in_proj",
    )(h, w_all)


LOG2E = math.log2(math.e)


def _attn_kernel(sink_ref, slope_ref, q_ref, kc_ref, vc_ref, kp_ref, vp_ref, o_ref, bias_ref, *, n_kv):
    n = pl.program_id(1)
    blk = WINDOW
    n_heads = n_kv * GQA_RATIO

    @pl.when(n == 0)
    def _():
        qi = lax.broadcasted_iota(jnp.int32, (blk, 2 * blk), 0)
        kj = lax.broadcasted_iota(jnp.int32, (blk, 2 * blk), 1)
        dist = qi + blk - kj
        in_window = (dist >= 0) & (dist < WINDOW)
        distf = dist.astype(F32)
        for first, valid in ((0, in_window & (kj >= blk)), (1, in_window)):
            for h in range(n_heads):
                bias_ref[first, h] = jnp.where(valid, (-LOG2E * slope_ref[h]) * distf, NEG_INF)

    later = jnp.minimum(n, 1)
    scale = HEAD_DIM ** -0.5 * LOG2E
    for hk in range(n_kv):
        ks = slice(hk * HEAD_DIM, (hk + 1) * HEAD_DIM)
        k = jnp.concatenate([kp_ref[:, ks], kc_ref[:, ks]], axis=0)
        v = jnp.concatenate([vp_ref[:, ks], vc_ref[:, ks]], axis=0)
        q_group = jnp.concatenate(
            [q_ref[:, (hk * GQA_RATIO + g) * HEAD_DIM:(hk * GQA_RATIO + g + 1) * HEAD_DIM] for g in range(GQA_RATIO)],
            axis=0)
        s_group = lax.dot_general(q_group, k, (((1,), (1,)), ((), ())), preferred_element_type=F32)
        for g in range(GQA_RATIO):
            h = hk * GQA_RATIO + g
            hs = slice(h * HEAD_DIM, (h + 1) * HEAD_DIM)
            s = s_group[g * blk:(g + 1) * blk] * scale + bias_ref[later, h]
            sink = sink_ref[h] * LOG2E
            mx = jnp.maximum(jnp.max(s, axis=-1, keepdims=True), sink)
            p = jnp.exp2(s - mx)
            denom = jnp.sum(p, axis=-1, keepdims=True) + jnp.exp2(sink - mx)
            out = jnp.dot(p.astype(BF16), v, preferred_element_type=F32) / denom
            o_ref[:, hs] = out.astype(o_ref.dtype)


def attention(proj, sinks, slopes, batch, seq, n_heads):
    m = proj.shape[0]
    n_kv = n_heads // GQA_RATIO
    qw, kw = n_heads * HEAD_DIM, n_kv * HEAD_DIM
    nb = seq // WINDOW
    k_blk, v_blk = qw // kw, qw // kw + 1
    cur = lambda b, n: b * nb + n
    prev = lambda b, n: b * nb + jnp.maximum(n - 1, 0)
    smem = pl.BlockSpec(memory_space=pltpu.SMEM)
    return pl.pallas_call(
        functools.partial(_attn_kernel, n_kv=n_kv), grid=(batch, nb),
        in_specs=[smem, smem,
                  pl.BlockSpec((WINDOW, qw), lambda b, n: (cur(b, n), 0)),
                  pl.BlockSpec((WINDOW, kw), lambda b, n: (cur(b, n), k_blk)),
                  pl.BlockSpec((WINDOW, kw), lambda b, n: (cur(b, n), v_blk)),
                  pl.BlockSpec((WINDOW, kw), lambda b, n: (prev(b, n), k_blk)),
                  pl.BlockSpec((WINDOW, kw), lambda b, n: (prev(b, n), v_blk))],
        out_specs=pl.BlockSpec((WINDOW, qw), lambda b, n: (cur(b, n), 0)),
        out_shape=jax.ShapeDtypeStruct((m, qw), BF16),
        scratch_shapes=[pltpu.VMEM((2, n_heads, WINDOW, 2 * WINDOW), F32)],
        compiler_params=_params("parallel", "arbitrary"), name="swa_attention",
    )(sinks, slopes, proj, proj, proj, proj, proj)


CONV_HALO = 32
SUBLANES = 8


def _conformer_kernel(a0_ref, a1_ref, g0_ref, g1_ref, dww_ref, dwb_ref, lng_ref, lnb_ref,
                      pw_ref, o_ref, ysh_ref, cbuf_ref, *, ts, width, rows_per_chunk):
    half = a0_ref.shape[1]
    ch = 2 * half
    buf_rows = ts + CONV_HALO

    @pl.when(pl.program_id(1) == 0)
    def _():
        ysh_ref[0, 0:CONV_HALO, :] = jnp.zeros((CONV_HALO, ch), F32)

    for c, (a_ref, g_ref) in enumerate(((a0_ref, g0_ref), (a1_ref, g1_ref))):
        a = a_ref[...].astype(F32)
        g = g_ref[...].astype(F32)
        ysh_ref[0, CONV_HALO:, c * half:(c + 1) * half] = a * jax.nn.sigmoid(g)

    y0 = ysh_ref[0]
    for b in range(1, SUBLANES):
        ysh_ref[b, SUBLANES:, :] = pltpu.roll(y0, b, axis=0)[SUBLANES:, :]

    lanes = 128

    n_aligned = -(-width // SUBLANES)
    back = SUBLANES * (n_aligned - 1)

    def conv_chunk(r, carry):
        r0 = pl.multiple_of(r * rows_per_chunk, rows_per_chunk)
        for c0 in range(0, ch, lanes):
            acc = jnp.broadcast_to(dwb_ref[:, c0:c0 + lanes], (rows_per_chunk, lanes))
            for b in range(SUBLANES):
                ext = ysh_ref[b, pl.ds(r0 + CONV_HALO - back, rows_per_chunk + back), c0:c0 + lanes]
                for a in range(n_aligned):
                    delay = SUBLANES * a + b
                    if delay < width:
                        k = width - 1 - delay
                        lo = back - SUBLANES * a
                        acc = acc + dww_ref[k:k + 1, c0:c0 + lanes] * ext[lo:lo + rows_per_chunk]
            cbuf_ref[pl.ds(r0, rows_per_chunk), c0:c0 + lanes] = acc
        return carry

    lax.fori_loop(0, ts // rows_per_chunk, conv_chunk, 0)
    ysh_ref[0, 0:CONV_HALO, :] = ysh_ref[0, ts:buf_rows, :]

    y = cbuf_ref[...]
    mu = jnp.mean(y, axis=-1, keepdims=True)
    yc = y - mu
    yn = yc * lax.rsqrt(jnp.mean(yc * yc, axis=-1, keepdims=True) + NORM_EPS)
    yn = yn * lng_ref[...] + lnb_ref[...]
    act = yn * jax.nn.sigmoid(yn)
    o_ref[...] = jnp.dot(act.astype(BF16), pw_ref[...], preferred_element_type=F32).astype(o_ref.dtype)


def conformer(proj, dw_w, dw_b, ln_g, ln_b, pw_all, layer, batch, seq, col0):
    m = proj.shape[0]
    width, ch = dw_w.shape
    half = ch // 2
    assert width - 1 <= CONV_HALO and col0 % half == 0
    ts = _tile(seq, 512)
    ns = seq // ts
    cb = col0 // half
    row = lambda b, s: b * ns + s
    col_spec = lambda j: pl.BlockSpec((ts, half), lambda b, s: (row(b, s), cb + j))
    full = lambda shape: pl.BlockSpec(shape, lambda b, s: (0, 0))
    return pl.pallas_call(
        functools.partial(_conformer_kernel, ts=ts, width=width, rows_per_chunk=64),
        grid=(batch, ns),
        in_specs=[col_spec(0), col_spec(1), col_spec(2), col_spec(3),
                  full((width, ch)), full((1, ch)), full((1, ch)), full((1, ch)),
                  pl.BlockSpec((None, ch, ch), lambda b, s: (layer, 0, 0))],
        out_specs=pl.BlockSpec((ts, ch), lambda b, s: (row(b, s), 0)),
        out_shape=jax.ShapeDtypeStruct((m, ch), BF16),
        scratch_shapes=[pltpu.VMEM((SUBLANES, ts + CONV_HALO, ch), F32), pltpu.VMEM((ts, ch), F32)],
        compiler_params=_params("parallel", "arbitrary"), name="conformer_conv",
    )(proj, proj, proj, proj, dw_w, dw_b.reshape(1, ch), ln_g.reshape(1, ch), ln_b.reshape(1, ch), pw_all)


def _retention_kernel(lg_ref, q_ref, k_ref, v_ref, g_ref, o_ref, state_ref, decay_ref, *, n_heads):
    c = RET_CHUNK
    scale = HEAD_DIM ** -0.5

    @pl.when(pl.program_id(1) == 0)
    def _():
        state_ref[...] = jnp.zeros(state_ref.shape, F32)
        row = lax.broadcasted_iota(jnp.int32, (c, c), 0)
        col = lax.broadcasted_iota(jnp.int32, (c, c), 1)
        rel = (row - col).astype(F32)
        for h in range(n_heads):
            decay_ref[h] = scale * jnp.where(row >= col, jnp.exp(lg_ref[h] * jnp.maximum(rel, 0.0)), 0.0)

    pos = lax.broadcasted_iota(jnp.int32, (c, 1), 0).astype(F32)
    chunk_len = jnp.full((1, 1), float(c), F32)
    for h in range(n_heads):
        hs = slice(h * HEAD_DIM, (h + 1) * HEAD_DIM)
        lg = lg_ref[h]
        q, k, v = q_ref[:, hs], k_ref[:, hs], v_ref[:, hs]
        s = lax.dot_general(q, k, (((1,), (1,)), ((), ())), preferred_element_type=F32)
        s = s * decay_ref[h]
        intra = jnp.dot(s.astype(BF16), v, preferred_element_type=F32)
        state = state_ref[h]
        q_decay = jnp.exp(lg * (pos + 1.0))
        cross = q_decay * jnp.dot(q, state.astype(BF16), preferred_element_type=F32)
        k_decay = jnp.exp(lg * (c - 1.0 - pos)) * scale
        kd = (k.astype(F32) * k_decay).astype(BF16)
        kv = lax.dot_general(kd, v, (((0,), (0,)), ((), ())), preferred_element_type=F32)
        state_ref[h] = state * jnp.exp(lg * chunk_len) + kv
        r = intra + cross
        mu = jnp.mean(r, axis=-1, keepdims=True)
        rc = r - mu
        rn = rc * lax.rsqrt(jnp.mean(rc * rc, axis=-1, keepdims=True) + NORM_EPS)
        gate = g_ref[:, hs].astype(F32)
        o_ref[:, hs] = (rn * (gate * jax.nn.sigmoid(gate))).astype(o_ref.dtype)


def retention(proj, log_gamma, batch, seq, n_heads, col0):
    m = proj.shape[0]
    w = n_heads * HEAD_DIM
    assert col0 % w == 0
    cb = col0 // w
    nc = seq // RET_CHUNK
    row = lambda b, n: b * nc + n
    spec = lambda j: pl.BlockSpec((RET_CHUNK, w), lambda b, n: (row(b, n), cb + j))
    return pl.pallas_call(
        functools.partial(_retention_kernel, n_heads=n_heads), grid=(batch, nc),
        in_specs=[pl.BlockSpec(memory_space=pltpu.SMEM), spec(0), spec(1), spec(2), spec(3)],
        out_specs=pl.BlockSpec((RET_CHUNK, w), lambda b, n: (row(b, n), 0)),
        out_shape=jax.ShapeDtypeStruct((m, w), BF16),
        scratch_shapes=[pltpu.VMEM((n_heads, HEAD_DIM, HEAD_DIM), F32),
                        pltpu.VMEM((n_heads, RET_CHUNK, RET_CHUNK), F32)],
        compiler_params=_params("parallel", "arbitrary"), name="retention",
    )(log_gamma, proj, proj, proj, proj)


def _mix_out_kernel(att_ref, conv_ref, ret_ref, w_ref, o_ref, wb_ref):
    _cast_weight_once(w_ref, wb_ref)
    ka, kc = att_ref.shape[1], conv_ref.shape[1]
    acc = jnp.dot(att_ref[...], wb_ref[0:ka, :], preferred_element_type=F32)
    acc += jnp.dot(conv_ref[...], wb_ref[ka:ka + kc, :], preferred_element_type=F32)
    acc += jnp.dot(ret_ref[...], wb_ref[ka + kc:, :], preferred_element_type=F32)
    o_ref[...] = acc.astype(o_ref.dtype)


def mix_out(att, conv, ret, w_all, layer):
    m = att.shape[0]
    _, k, n = w_all.shape
    tm, tn = _tile(m, 1024), _tile(n, 512)
    lhs = lambda a: pl.BlockSpec((tm, a.shape[1]), lambda j, i: (i, 0))
    return pl.pallas_call(
        _mix_out_kernel, grid=(n // tn, m // tm),
        in_specs=[lhs(att), lhs(conv), lhs(ret),
                  pl.BlockSpec((None, k, tn), lambda j, i: (layer, 0, j))],
        out_specs=pl.BlockSpec((tm, tn), lambda j, i: (i, j)),
        out_shape=jax.ShapeDtypeStruct((m, n), BF16),
        scratch_shapes=[pltpu.VMEM((k, tn), BF16)],
        compiler_params=_params("parallel", "arbitrary"), name="mix_out",
    )(att, conv, ret, w_all)


FFN_CARRY = 8


GELU_C1 = math.sqrt(2.0 / math.pi)
GELU_C2 = GELU_C1 * 0.044715


def _gelu_tanh(x):
    g = 0.5 * x
    return g + g * jnp.tanh(x * (GELU_C1 + GELU_C2 * (x * x)))


def _ffn_in_kernel(h_ref, wg_ref, wv_ref, dwg_ref, dwv_ref, bg_ref, bv_ref, o_ref,
                   wgb_ref, wvb_ref, ug_ref, uv_ref, *, tm, width, tiles_per_seq):
    i = pl.program_id(1)
    _cast_weight_once(wg_ref, wgb_ref)
    _cast_weight_once(wv_ref, wvb_ref)

    seq_start = (i % tiles_per_seq) == 0

    @pl.when(seq_start)
    def _():
        ug_ref[0:FFN_CARRY, :] = jnp.zeros((FFN_CARRY, ug_ref.shape[1]), F32)
        uv_ref[0:FFN_CARRY, :] = jnp.zeros((FFN_CARRY, uv_ref.shape[1]), F32)

    @pl.when(jnp.logical_not(seq_start))
    def _():
        ug_ref[0:FFN_CARRY, :] = ug_ref[tm:tm + FFN_CARRY, :]
        uv_ref[0:FFN_CARRY, :] = uv_ref[tm:tm + FFN_CARRY, :]

    h = h_ref[...]
    ug_ref[FFN_CARRY:, :] = jnp.dot(h, wgb_ref[...], preferred_element_type=F32)
    uv_ref[FFN_CARRY:, :] = jnp.dot(h, wvb_ref[...], preferred_element_type=F32)

    def conv(u_ref, dw_ref, b_ref):
        u = u_ref[...]
        acc = dw_ref[0:1, :] * u
        for k in range(1, width):
            acc = dw_ref[k:k + 1, :] * u + pltpu.roll(acc, 1, axis=0)
        return acc[FFN_CARRY:, :] + b_ref[...]

    gate = conv(ug_ref, dwg_ref, bg_ref)
    val = conv(uv_ref, dwv_ref, bv_ref)
    o_ref[...] = (_gelu_tanh(gate) * val).astype(o_ref.dtype)


def ffn_in(h, w_all, layer, dw_w, dw_b, seq):
    m, k = h.shape
    width, n2 = dw_w.shape
    d_ff = n2 // 2
    assert width - 1 <= FFN_CARRY
    tm, tn = _tile(seq, 1024), _tile(d_ff, 256)
    nj = d_ff // tn
    u_scratch = pltpu.VMEM((tm + FFN_CARRY, tn), F32)
    return pl.pallas_call(
        functools.partial(_ffn_in_kernel, tm=tm, width=width, tiles_per_seq=seq // tm),
        grid=(nj, m // tm),
        in_specs=[pl.BlockSpec((tm, k), lambda j, i: (i, 0)),
                  pl.BlockSpec((None, k, tn), lambda j, i: (layer, 0, j)),
                  pl.BlockSpec((None, k, tn), lambda j, i: (layer, 0, j + nj)),
                  pl.BlockSpec((width, tn), lambda j, i: (0, j)),
                  pl.BlockSpec((width, tn), lambda j, i: (0, j + nj)),
                  pl.BlockSpec((1, tn), lambda j, i: (0, j)),
                  pl.BlockSpec((1, tn), lambda j, i: (0, j + nj))],
        out_specs=pl.BlockSpec((tm, tn), lambda j, i: (i, j)),
        out_shape=jax.ShapeDtypeStruct((m, d_ff), BF16),
        scratch_shapes=[pltpu.VMEM((k, tn), BF16), pltpu.VMEM((k, tn), BF16), u_scratch, u_scratch],
        compiler_params=_params("parallel", "arbitrary"), name="ffn_in",
    )(h, w_all, w_all, dw_w, dw_w, dw_b.reshape(1, n2), dw_b.reshape(1, n2))


def ffn_out(a, w_all, layer):
    m, k = a.shape
    n = w_all.shape[2]
    tm, tn = _tile(m, 512), _tile(n, 512)
    return pl.pallas_call(
        _matmul_kernel, grid=(m // tm, n // tn),
        in_specs=[pl.BlockSpec((tm, k), lambda i, j: (i, 0)),
                  pl.BlockSpec((None, k, tn), lambda i, j: (layer, 0, j))],
        out_specs=pl.BlockSpec((tm, tn), lambda i, j: (i, j)),
        out_shape=jax.ShapeDtypeStruct((m, n), BF16),
        compiler_params=_params("parallel", "arbitrary"), name="ffn_out",
    )(a, w_all)


CAST_BLOCK_BYTES = 8 * 1024 * 1024


def _cast_kernel(w_ref, o_ref):
    o_ref[...] = w_ref[...].astype(o_ref.dtype)


def cast_bf16(w_all):
    depth, k, n = w_all.shape
    tk = _tile(k, max(16, CAST_BLOCK_BYTES // (4 * n)))
    spec = pl.BlockSpec((None, tk, n), lambda l, i: (l, i, 0))
    return pl.pallas_call(
        _cast_kernel, grid=(depth, k // tk), in_specs=[spec], out_specs=spec,
        out_shape=jax.ShapeDtypeStruct(w_all.shape, BF16),
        compiler_params=_params("parallel", "parallel"), name="cast_bf16",
    )(w_all)


def _pow2_slopes(n):
    start = 2.0 ** (-8.0 / n)
    return [start ** (i + 1) for i in range(n)]


def _alibi_slopes(n):
    if math.log2(n).is_integer():
        return _pow2_slopes(n)
    c = 2 ** math.floor(math.log2(n))
    return _pow2_slopes(c) + _alibi_slopes(2 * c)[0::2][: n - c]


def kernel(x, mix_pre_g, mix_post_g, ffn_pre_g, ffn_post_g, w_in, att_sinks, conv_dw_w, conv_dw_b,
           conv_ln_g, conv_ln_b, conv_pw_w, w_out, ffn_w_in, ffn_dw_w, ffn_dw_b, ffn_w_out):
    b, s, d = x.shape
    depth = w_in.shape[0]
    att_heads = att_sinks.shape[1]
    conv_ch = conv_dw_b.shape[1]
    att_q_cols = att_heads * HEAD_DIM
    att_kv_cols = att_q_cols // GQA_RATIO
    ret_heads = (w_in.shape[2] - att_q_cols - 2 * att_kv_cols - 2 * conv_ch) // (4 * HEAD_DIM)
    conv_col0 = att_q_cols + 2 * att_kv_cols
    ret_col0 = conv_col0 + 2 * conv_ch
    assert s % WINDOW == 0 and s % RET_CHUNK == 0

    slopes = jnp.asarray(_alibi_slopes(att_heads), dtype=F32)
    log_gamma = jnp.log1p(-jnp.exp2(-5.0 - jnp.arange(ret_heads, dtype=F32)))

    conv_pw_bf16 = cast_bf16(conv_pw_w)
    ffn_w_out_bf16 = cast_bf16(ffn_w_out)

    xf = x.reshape(b * s, d)
    y = None
    for l in range(depth):
        if l == 0:
            h = prenorm(xf, mix_pre_g[l])
        else:
            xf, h = resid_prenorm(xf, y, ffn_post_g[l - 1], mix_pre_g[l])
        proj = in_proj(h, w_in, l)
        att = attention(proj, att_sinks[l], slopes, b, s, att_heads)
        conv = conformer(proj, conv_dw_w[l], conv_dw_b[l], conv_ln_g[l], conv_ln_b[l],
                         conv_pw_bf16, l, b, s, conv_col0)
        ret = retention(proj, log_gamma, b, s, ret_heads, ret_col0)
        y = mix_out(att, conv, ret, w_out, l)
        xf, h = resid_prenorm(xf, y, mix_post_g[l], ffn_pre_g[l])
        a = ffn_in(h, ffn_w_in, l, ffn_dw_w[l], ffn_dw_b[l], s)
        y = ffn_out(a, ffn_w_out_bf16, l)
    xf = resid(xf, y, ffn_post_g[depth - 1])
    return xf.reshape(b, s, d)
```
